```python
import jax
import jax.numpy as jnp
from jax import lax
import numpy as np

D_MODEL = 2048
BATCH = 1
SEQ = 8192
DEPTH = 2
DEC_BATCH = 4
DEC_SEQ = 8192
PAST_LEN = 128

MIX_WIDTH = D_MODEL
GROUP_WIDTH = MIX_WIDTH // 4
FNET_GROUPS = 4
FNET_GROUP_DIM = GROUP_WIDTH // FNET_GROUPS
GLA_HEADS = 4
GLA_DK = GROUP_WIDTH // (2 * GLA_HEADS)
GLA_DV = GROUP_WIDTH // GLA_HEADS
GLA_RANK = 16
GLA_TAU = 16.0
GLA_CHUNK = 64
LRU_WIDTH = GROUP_WIDTH
LRU_BLOCKS = 8
LRU_BLOCK_DIM = LRU_WIDTH // LRU_BLOCKS
LRU_C = 8.0
LRU_TAPS = 4
CONF_WIDTH = GROUP_WIDTH
CONF_TAPS = 31
D_FF = 5632
N_SUB = 3
EPS = 1e-6
MIX_IN_SIZES = (GROUP_WIDTH, GLA_HEADS * GLA_DK, GLA_HEADS * GLA_DK, GLA_HEADS * GLA_DV, GLA_HEADS * GLA_DV, 2 * GLA_RANK, LRU_WIDTH, LRU_WIDTH, 2 * CONF_WIDTH)
MIX_IN_WIDTH = sum(MIX_IN_SIZES)

kernel_name = 'hybrid_bidir_encoder_two_groups'


def rms_norm(x, g):
    xf = x.astype(jnp.float32)
    y = xf * lax.rsqrt(jnp.mean(xf * xf, axis=-1, keepdims=True) + EPS)
    return (y * g.astype(jnp.float32)).astype(x.dtype)


def layer_norm(x, g, b):
    xf = x.astype(jnp.float32)
    xc = xf - jnp.mean(xf, axis=-1, keepdims=True)
    y = xc * lax.rsqrt(jnp.mean(xc * xc, axis=-1, keepdims=True) + EPS)
    return (y * g.astype(jnp.float32) + b.astype(jnp.float32)).astype(x.dtype)


def depthwise_conv(x, w, pad_left, pad_right):
    return lax.conv_general_dilated(
        x, w[:, None, :].astype(x.dtype), window_strides=(1,),
        padding=[(pad_left, pad_right)],
        dimension_numbers=('NWC', 'WIO', 'NWC'),
        feature_group_count=x.shape[-1])


def swiglu(h, w_in, w_out):
    up, gate = jnp.split(h @ w_in, 2, axis=-1)
    return (jax.nn.silu(gate) * up) @ w_out


def flip(t):
    return jnp.flip(t, axis=1)


def fourier_mix(u):
    B, S, _ = u.shape
    uf = u.astype(jnp.float32).reshape(B, S, FNET_GROUPS, FNET_GROUP_DIM)
    y = jnp.fft.fft2(uf, axes=(1, 3), norm='ortho').real
    return y.reshape(B, S, GROUP_WIDTH).astype(u.dtype)


def gla_scan(q, k, v, log_a):
    B, S, H, DK = q.shape
    DV = v.shape[-1]
    n = S // GLA_CHUNK

    def to_chunks(t):
        return t.astype(jnp.float32).reshape(B, n, GLA_CHUNK, H, t.shape[-1]).transpose(1, 0, 3, 2, 4)

    qc, kc, vc, gc = to_chunks(q), to_chunks(k), to_chunks(v), to_chunks(log_a)
    mask = jnp.tril(jnp.ones((GLA_CHUNK, GLA_CHUNK), dtype=bool))[:, :, None]

    def step(state, inp):
        qi, ki, vi, gi = inp
        b = jnp.cumsum(gi, axis=2)
        o_inter = jnp.einsum('bhtk,bhkv->bhtv', qi * jnp.exp(b), state)
        diff = b[:, :, :, None, :] - b[:, :, None, :, :]
        decay = jnp.exp(jnp.where(mask, diff, -jnp.inf))
        scores = jnp.einsum('bhtk,bhsk,bhtsk->bhts', qi, ki, decay)
        o_intra = jnp.einsum('bhts,bhsv->bhtv', scores, vi)
        b_last = b[:, :, -1:, :]
        new_state = jnp.exp(b_last[:, :, 0, :])[..., None] * state + jnp.einsum(
            'bhsk,bhsv->bhkv', ki * jnp.exp(b_last - b), vi)
        return new_state, o_inter + o_intra

    state0 = jnp.zeros((B, H, DK, DV), jnp.float32)
    _, out = lax.scan(step, state0, (qc, kc, vc, gc))
    return out.transpose(1, 0, 3, 2, 4).reshape(B, S, H, DV)


def rglru_direction(x, conv_w, conv_b, w_a, b_a, w_x, b_x, lam):
    B, S, W = x.shape
    xc = depthwise_conv(x, conv_w, LRU_TAPS - 1, 0) + conv_b
    xb = xc.reshape(B, S, LRU_BLOCKS, LRU_BLOCK_DIM)
    r = jax.nn.sigmoid((jnp.einsum('bsni,nij->bsnj', xb, w_a).reshape(B, S, W) + b_a).astype(jnp.float32))
    i = jax.nn.sigmoid((jnp.einsum('bsni,nij->bsnj', xb, w_x).reshape(B, S, W) + b_x).astype(jnp.float32))
    log_a = -LRU_C * r * jax.nn.softplus(-lam.astype(jnp.float32))
    a = jnp.exp(log_a)
    u = jnp.sqrt(-jnp.expm1(2.0 * log_a)) * (i * xc.astype(jnp.float32))

    def combine(left, right):
        a_l, h_l = left
        a_r, h_r = right
        return a_l * a_r, a_r * h_l + h_r

    _, hs = lax.associative_scan(combine, (a, u), axis=1)
    return hs.astype(x.dtype)


def hybrid_mixer(h, w_mix_in, gla_w_alpha, gla_b_alpha, gla_norm_g,
                 lru_conv_w, lru_conv_b, lru_w_a, lru_b_a, lru_w_x, lru_b_x, lru_lambda,
                 conf_dw_w, conf_dw_b, conf_ln_g, conf_ln_b, w_mix_out):
    B, S, _ = h.shape
    offsets = [sum(MIX_IN_SIZES[:j + 1]) for j in range(len(MIX_IN_SIZES) - 1)]
    f_in, q, k, v, o_gate, a_lr, r_in, r_gate, c_in = jnp.split(h @ w_mix_in, offsets, axis=-1)

    y_f = fourier_mix(f_in)

    q = q.reshape(B, S, GLA_HEADS, GLA_DK) * (GLA_DK ** -0.5)
    k = k.reshape(B, S, GLA_HEADS, GLA_DK)
    v = v.reshape(B, S, GLA_HEADS, GLA_DV)
    z = jnp.einsum('bsdr,drk->bsdk', a_lr.reshape(B, S, 2, GLA_RANK), gla_w_alpha) + gla_b_alpha
    log_a = (jax.nn.log_sigmoid(z.astype(jnp.float32)) / GLA_TAU).reshape(B, S, 2, GLA_HEADS, GLA_DK)
    o_fwd = gla_scan(q, k, v, log_a[:, :, 0])
    o_bwd = flip(gla_scan(flip(q), flip(k), flip(v), flip(log_a[:, :, 1])))
    o = rms_norm((o_fwd + o_bwd).astype(h.dtype), gla_norm_g) * jax.nn.silu(o_gate.reshape(B, S, GLA_HEADS, GLA_DV))
    y_g = o.reshape(B, S, GROUP_WIDTH)

    h_fwd = rglru_direction(r_in, lru_conv_w[0], lru_conv_b[0], lru_w_a[0], lru_b_a[0],
                            lru_w_x[0], lru_b_x[0], lru_lambda[0])
    h_bwd = flip(rglru_direction(flip(r_in), lru_conv_w[1], lru_conv_b[1], lru_w_a[1], lru_b_a[1],
                                 lru_w_x[1], lru_b_x[1], lru_lambda[1]))
    y_r = (h_fwd + h_bwd) * jax.nn.gelu(r_gate)

    c_val, c_g = jnp.split(c_in, 2, axis=-1)
    u = c_val * jax.nn.sigmoid(c_g)
    u = depthwise_conv(u, conf_dw_w, CONF_TAPS // 2, CONF_TAPS // 2) + conf_dw_b
    y_c = jax.nn.silu(layer_norm(u, conf_ln_g, conf_ln_b))

    return jnp.concatenate([y_f, y_g, y_r, y_c], axis=-1) @ w_mix_out


def encoder_layer(x, c, w_ada, b_ada, g_pre, g_post, ffn1_w_in, ffn1_w_out, ffn2_w_in, ffn2_w_out,
                  w_mix_in, gla_w_alpha, gla_b_alpha, gla_norm_g,
                  lru_conv_w, lru_conv_b, lru_w_a, lru_b_a, lru_w_x, lru_b_x, lru_lambda,
                  conf_dw_w, conf_dw_b, conf_ln_g, conf_ln_b, w_mix_out):
    B = x.shape[0]
    mod = (jax.nn.silu(c) @ w_ada + b_ada).reshape(B, N_SUB, 3, D_MODEL)[:, :, :, None, :]

    def pre(t, j):
        return rms_norm(t, g_pre[j]) * (1.0 + mod[:, j, 1]) + mod[:, j, 0]

    def post(t, out, j, w):
        return t + w * mod[:, j, 2] * rms_norm(out, g_post[j])

    x = post(x, swiglu(pre(x, 0), ffn1_w_in, ffn1_w_out), 0, 0.5)
    x = post(x, hybrid_mixer(pre(x, 1), w_mix_in, gla_w_alpha, gla_b_alpha, gla_norm_g,
                             lru_conv_w, lru_conv_b, lru_w_a, lru_b_a, lru_w_x, lru_b_x, lru_lambda,
                             conf_dw_w, conf_dw_b, conf_ln_g, conf_ln_b, w_mix_out), 1, 1.0)
    x = post(x, swiglu(pre(x, 2), ffn2_w_in, ffn2_w_out), 2, 0.5)
    return x


def setup_inputs(seed: int = 0) -> dict:
    key = jax.random.key(seed)
    L, D = DEPTH, D_MODEL
    ks = jax.random.split(key, 28)

    def nrm(j, shape, scale):
        return scale * jax.random.normal(ks[j], shape, jnp.float32)

    a_c = jax.random.uniform(ks[22], (L, 2, LRU_WIDTH), jnp.float32, 0.9, 0.999)
    sig = a_c ** (1.0 / LRU_C)
    lru_lambda = jnp.log(sig) - jnp.log1p(-sig)
    return {
        'x_prompt': nrm(0, (BATCH, SEQ, D), 1.0),
        'x_sample': nrm(1, (DEC_BATCH, DEC_SEQ, D), 1.0),
        'c_prompt': nrm(2, (BATCH, D), 1.0),
        'c_sample': nrm(3, (DEC_BATCH, D), 1.0),
        'w_ada': nrm(4, (L, D, N_SUB * 3 * D), 0.5 * D ** -0.5),
        'b_ada': nrm(5, (L, N_SUB * 3 * D), 0.01),
        'g_pre': 1.0 + nrm(6, (L, N_SUB, D), 0.02),
        'g_post': 1.0 + nrm(7, (L, N_SUB, D), 0.02),
        'ffn1_w_in': nrm(8, (L, D, 2 * D_FF), D ** -0.5),
        'ffn1_w_out': nrm(9, (L, D_FF, D), D_FF ** -0.5),
        'ffn2_w_in': nrm(10, (L, D, 2 * D_FF), D ** -0.5),
        'ffn2_w_out': nrm(11, (L, D_FF, D), D_FF ** -0.5),
        'w_mix_in': nrm(12, (L, D, MIX_IN_WIDTH), D ** -0.5),
        'gla_w_alpha': nrm(13, (L, 2, GLA_RANK, GLA_HEADS * GLA_DK), GLA_RANK ** -0.5),
        'gla_b_alpha': nrm(14, (L, 2, GLA_HEADS * GLA_DK), 0.1),
        'gla_norm_g': 1.0 + nrm(15, (L, GLA_DV), 0.02),
        'lru_conv_w': nrm(16, (L, 2, LRU_TAPS, LRU_WIDTH), LRU_TAPS ** -0.5),
        'lru_conv_b': nrm(17, (L, 2, LRU_WIDTH), 0.01),
        'lru_w_a': nrm(18, (L, 2, LRU_BLOCKS, LRU_BLOCK_DIM, LRU_BLOCK_DIM), LRU_BLOCK_DIM ** -0.5),
        'lru_b_a': nrm(19, (L, 2, LRU_WIDTH), 0.01),
        'lru_w_x': nrm(20, (L, 2, LRU_BLOCKS, LRU_BLOCK_DIM, LRU_BLOCK_DIM), LRU_BLOCK_DIM ** -0.5),
        'lru_b_x': nrm(21, (L, 2, LRU_WIDTH), 0.01),
        'lru_lambda': lru_lambda,
        'conf_dw_w': nrm(23, (L, CONF_TAPS, CONF_WIDTH), CONF_TAPS ** -0.5),
        'conf_dw_b': nrm(24, (L, CONF_WIDTH), 0.01),
        'conf_ln_g': 1.0 + nrm(25, (L, CONF_WIDTH), 0.02),
        'conf_ln_b': nrm(26, (L, CONF_WIDTH), 0.01),
        'w_mix_out': nrm(27, (L, MIX_WIDTH, D), MIX_WIDTH ** -0.5),
    }


def reference(x_prompt, x_sample, c_prompt, c_sample, w_ada, b_ada, g_pre, g_post,
              ffn1_w_in, ffn1_w_out, ffn2_w_in, ffn2_w_out, w_mix_in,
              gla_w_alpha, gla_b_alpha, gla_norm_g,
              lru_conv_w, lru_conv_b, lru_w_a, lru_b_a, lru_w_x, lru_b_x, lru_lambda,
              conf_dw_w, conf_dw_b, conf_ln_g, conf_ln_b, w_mix_out):
    def run(x, c):
        for l in range(DEPTH):
            x = encoder_layer(x, c, w_ada[l], b_ada[l], g_pre[l], g_post[l],
                              ffn1_w_in[l], ffn1_w_out[l], ffn2_w_in[l], ffn2_w_out[l],
                              w_mix_in[l], gla_w_alpha[l], gla_b_alpha[l], gla_norm_g[l],
                              lru_conv_w[l], lru_conv_b[l], lru_w_a[l], lru_b_a[l],
                              lru_w_x[l], lru_b_x[l], lru_lambda[l],
                              conf_dw_w[l], conf_dw_b[l], conf_ln_g[l], conf_ln_b[l], w_mix_out[l])
        return x

    y_prompt = run(x_prompt, c_prompt)
    y_sample = run(x_sample, c_sample)
    return (y_prompt, y_sample)
```

```python
import functools
import math

import numpy as np
import jax
import jax.numpy as jnp
from jax import lax
from jax.experimental import pallas as pl
from jax.experimental.pallas import tpu as pltpu

F32 = jnp.float32
BF16 = jnp.bfloat16

EPS = 1e-6
N_SUB = 3
GROUP_WIDTH = 512
FNET_GROUPS = 4
FNET_GROUP_DIM = 128
GLA_HEADS = 4
GLA_DK = 64
GLA_DV = 128
GLA_RANK = 16
GLA_TAU = 16.0
GLA_CHUNK = 64
LRU_BLOCKS = 8
LRU_BLOCK_DIM = 64
LRU_C = 8.0
LRU_TAPS = 4
CONF_TAPS = 31
CONF_HALO = 16
LRU_HALO = 8

V7X_VMEM_BYTES = 64 * 1024 * 1024
VMEM_LIMIT = 52 * 1024 * 1024

NEG_BIG = -1e30


def _cparams(sem):
    return pltpu.CompilerParams(dimension_semantics=sem, vmem_limit_bytes=VMEM_LIMIT)


def _sigmoid(x):
    return 1.0 / (1.0 + jnp.exp(-x))


def _silu(x):
    return x * _sigmoid(x)


def _rms(x, g):
    return x * lax.rsqrt(jnp.mean(x * x, axis=-1, keepdims=True) + EPS) * g


def _dot(a, b):
    return jnp.dot(a, b, preferred_element_type=F32)


def _dot_nt(a, b):
    return lax.dot_general(a, b, (((1,), (1,)), ((), ())), preferred_element_type=F32)


def _dot_tn(a, b):
    return lax.dot_general(a, b, (((0,), (0,)), ((), ())), preferred_element_type=F32)


def _mod_kernel(c_ref, w_ref, b_ref, o_ref):
    c = c_ref[...]
    o_ref[...] = jnp.dot(_silu(c), w_ref[...], preferred_element_type=F32,
                         precision=lax.Precision.HIGHEST) + b_ref[...]


def _adaln_mod(c_pad, w_ada, b_ada):
    L, D, W = w_ada.shape
    NP = c_pad.shape[0]
    tn = 1024
    return pl.pallas_call(
        _mod_kernel,
        grid=(L, W // tn),
        in_specs=[
            pl.BlockSpec((NP, D), lambda l, j: (0, 0)),
            pl.BlockSpec((None, D, tn), lambda l, j: (l, 0, j)),
            pl.BlockSpec((None, 1, tn), lambda l, j: (l, 0, j)),
        ],
        out_specs=pl.BlockSpec((None, NP, tn), lambda l, j: (l, 0, j)),
        out_shape=jax.ShapeDtypeStruct((L, NP, W), F32),
        compiler_params=_cparams(("arbitrary", "arbitrary")),
        name="adaln_mod",
    )(c_pad, w_ada, b_ada.reshape(L, 1, W))


def _ffn_kernel(x_ref, sh_ref, sc_ref, gt_ref, gpre_ref, gpost_ref, wu_ref, wg_ref, wo_ref,
                o_ref, xn_ref, acc_ref, *, res_w):
    f = pl.program_id(2)

    @pl.when(f == 0)
    def _():
        x = x_ref[...]
        xn = _rms(x, gpre_ref[...]) * (1.0 + sc_ref[...]) + sh_ref[...]
        xn_ref[...] = xn.astype(BF16)
        acc_ref[...] = jnp.zeros_like(acc_ref)

    xn = xn_ref[...]
    up = _dot(xn, wu_ref[...])
    gate = _dot(xn, wg_ref[...])
    act = (_silu(gate) * up).astype(BF16)
    acc_ref[...] += _dot(act, wo_ref[...])

    @pl.when(f == pl.num_programs(2) - 1)
    def _():
        o = acc_ref[...]
        o_ref[...] = x_ref[...] + (res_w * gt_ref[...]) * _rms(o, gpost_ref[...])


def _ffn(x, mod_l, sub, g_pre, g_post, w_in, w_out, res_w):
    N, S, D = x.shape
    FF = w_out.shape[0]
    tm = min(512, S)
    tf = 512
    nf = FF // tf
    mspec = lambda r: pl.BlockSpec((None, None, 1, D), lambda n, i, f: (n, 3 * sub + r, 0, 0))
    return pl.pallas_call(
        functools.partial(_ffn_kernel, res_w=res_w),
        grid=(N, S // tm, nf),
        in_specs=[
            pl.BlockSpec((None, tm, D), lambda n, i, f: (n, i, 0)),
            mspec(0), mspec(1), mspec(2),
            pl.BlockSpec((1, D), lambda n, i, f: (0, 0)),
            pl.BlockSpec((1, D), lambda n, i, f: (0, 0)),
            pl.BlockSpec((D, tf), lambda n, i, f: (0, f)),
            pl.BlockSpec((D, tf), lambda n, i, f: (0, f + nf)),
            pl.BlockSpec((tf, D), lambda n, i, f: (f, 0)),
        ],
        out_specs=pl.BlockSpec((None, tm, D), lambda n, i, f: (n, i, 0)),
        out_shape=jax.ShapeDtypeStruct((N, S, D), F32),
        scratch_shapes=[pltpu.VMEM((tm, D), BF16), pltpu.VMEM((tm, D), F32)],
        compiler_params=_cparams(("arbitrary", "arbitrary", "arbitrary")),
        name="ffn",
    )(x, mod_l, mod_l, mod_l, g_pre.reshape(1, D), g_post.reshape(1, D), w_in, w_in, w_out)


def _mix_in_kernel(x_ref, sh_ref, sc_ref, gpre_ref, wf_ref, wg_ref, wr_ref, wc_ref, cs_ref,
                   xcs_ref, pg_ref, pr_ref, pc_ref):
    x = x_ref[...]
    xn = (_rms(x, gpre_ref[...]) * (1.0 + sc_ref[...]) + sh_ref[...]).astype(BF16)
    f_in = _dot(xn, wf_ref[...]).astype(BF16)
    xcs_ref[...] = _dot(f_in, cs_ref[...]).astype(BF16)
    pg_ref[...] = _dot(xn, wg_ref[...])
    pr_ref[...] = _dot(xn, wr_ref[...])
    pc_ref[...] = _dot(xn, wc_ref[...])


def _mix_in(x, mod_l, g_pre, wf, wg, wr, wc, cs128):
    N, S, D = x.shape
    tm = min(256, S)
    sub = 1
    mspec = lambda r: pl.BlockSpec((None, None, 1, D), lambda n, i: (n, 3 * sub + r, 0, 0))
    const = lambda shape: pl.BlockSpec(shape, lambda n, i: (0, 0), pipeline_mode=pl.Buffered(1))
    tok = lambda w: pl.BlockSpec((None, tm, w), lambda n, i: (n, i, 0))
    wgw, wrw, wcw = wg.shape[1], wr.shape[1], wc.shape[1]
    return pl.pallas_call(
        _mix_in_kernel,
        grid=(N, S // tm),
        in_specs=[tok(D), mspec(0), mspec(1), const((1, D)),
                  const(wf.shape), const(wg.shape), const(wr.shape), const(wc.shape),
                  const(cs128.shape)],
        out_specs=[tok(2 * GROUP_WIDTH), tok(wgw), tok(wrw), tok(wcw)],
        out_shape=[jax.ShapeDtypeStruct((N, S, 2 * GROUP_WIDTH), BF16),
                   jax.ShapeDtypeStruct((N, S, wgw), F32),
                   jax.ShapeDtypeStruct((N, S, wrw), F32),
                   jax.ShapeDtypeStruct((N, S, wcw), F32)],
        compiler_params=_cparams(("arbitrary", "arbitrary")),
        name="mix_in",
    )(x, mod_l, mod_l, g_pre.reshape(1, D), wf, wg, wr, wc, cs128)


def _seq_dft_kernel(t_ref, x_ref, o_ref):
    S = x_ref.shape[0]
    W = o_ref.shape[-1]
    y = _dot(t_ref[:, pl.ds(0, S)], x_ref[:, pl.ds(0, W)])
    y += _dot(t_ref[:, pl.ds(S, S)], x_ref[:, pl.ds(W, W)])
    o_ref[...] = y.astype(o_ref.dtype)


def _seq_dft(xcs, table):
    N, S, W2 = xcs.shape
    tr = min(256, S)
    return pl.pallas_call(
        _seq_dft_kernel,
        grid=(N, S // tr),
        in_specs=[
            pl.BlockSpec((tr, 2 * S), lambda n, i: (i, 0)),
            pl.BlockSpec((None, S, W2), lambda n, i: (n, 0, 0), pipeline_mode=pl.Buffered(1)),
        ],
        out_specs=pl.BlockSpec((None, tr, W2 // 2), lambda n, i: (n, i, 0)),
        out_shape=jax.ShapeDtypeStruct((N, S, W2 // 2), BF16),
        compiler_params=_cparams(("arbitrary", "arbitrary")),
        name="seq_dft",
    )(table, xcs)


def _gla_tables(rev):
    C = GLA_CHUNK
    idx = np.arange(C)
    if not rev:
        cum = (idx[None, :] <= idx[:, None]).astype(np.float32)
    else:
        cum = (idx[None, :] >= idx[:, None]).astype(np.float32)
    mats = []
    m = C // 2
    while m >= 1:
        base = (idx // (2 * m)) * (2 * m)
        piv = base + (m if rev else m - 1)
        mats.append(cum - cum[piv, :])
        m //= 2
    mats.append(cum)
    return np.concatenate(mats, axis=0)


def _split3(x):
    hi = x.astype(BF16)
    r1 = x - hi.astype(F32)
    mid = r1.astype(BF16)
    lo = (r1 - mid.astype(F32)).astype(BF16)
    return hi, mid, lo


def _gla_kernel(q_ref, k_ref, v_ref, alr_ref, og_ref, oth_ref, m_ref, wa_ref, ba_ref, g_ref,
                o_ref, st_ref, *, rev, final, nchunks):
    C, H, DK, DV = GLA_CHUNK, GLA_HEADS, GLA_DK, GLA_DV
    HK = H * DK

    @pl.when(pl.program_id(1) == 0)
    def _():
        st_ref[...] = jnp.zeros_like(st_ref)

    row = lax.broadcasted_iota(jnp.int32, (C, HK), 0)
    t_i = lax.broadcasted_iota(jnp.int32, (C, C), 0)
    s_i = lax.broadcasted_iota(jnp.int32, (C, C), 1)
    mstack = m_ref[...]
    d0 = (GLA_RANK if rev else 0)

    def chunk(ci, carry):
        cidx = (nchunks - 1 - ci) if rev else ci
        r0 = pl.multiple_of(cidx * C, C)
        rows = pl.ds(r0, C)
        q = q_ref[rows, :] * (DK ** -0.5)
        k = k_ref[rows, :]
        v = v_ref[rows, :].astype(BF16)
        a16 = alr_ref[rows, :][:, d0:d0 + GLA_RANK]
        z = jnp.dot(a16, wa_ref[...], preferred_element_type=F32,
                    precision=lax.Precision.HIGHEST) + ba_ref[...]
        la = (jnp.minimum(z, 0.0) - jnp.log1p(jnp.exp(-jnp.abs(z)))) / GLA_TAU
        hi, mid, lo = _split3(la)
        e_all = _dot(mstack, hi) + _dot(mstack, mid) + _dot(mstack, lo)

        a_h = [jnp.zeros((C, C), F32) for _ in range(H)]
        m = C // 2
        lvl = 0
        while m >= 1:
            d = e_all[lvl * C:(lvl + 1) * C]
            upper = (row & (2 * m - 1)) >= m
            q_act, k_act = (jnp.logical_not(upper), upper) if rev else (upper, jnp.logical_not(upper))
            qh = (q * jnp.exp(jnp.where(q_act, d, NEG_BIG))).astype(BF16)
            kh = (k * jnp.exp(jnp.where(k_act, -d, NEG_BIG))).astype(BF16)
            sh = int(math.log2(2 * m))
            same = (t_i >> sh) == (s_i >> sh)
            for h in range(H):
                p = _dot_nt(qh[:, h * DK:(h + 1) * DK], kh[:, h * DK:(h + 1) * DK])
                a_h[h] = a_h[h] + jnp.where(same, p, 0.0)
            m //= 2
            lvl += 1
        qb, kb = q.astype(BF16), k.astype(BF16)
        diag = t_i == s_i
        for h in range(H):
            p = _dot_nt(qb[:, h * DK:(h + 1) * DK], kb[:, h * DK:(h + 1) * DK])
            a_h[h] = a_h[h] + jnp.where(diag, p, 0.0)

        b = e_all[6 * C:7 * C]
        b_edge = b[0:1, :] if rev else b[C - 1:C, :]
        qi = (q * jnp.exp(b)).astype(BF16)
        kd = (k * jnp.exp(b_edge - b)).astype(BF16)
        dec = jnp.exp(b_edge)
        outs = []
        for h in range(H):
            st = st_ref[h]
            vh = v[:, h * DV:(h + 1) * DV]
            o = _dot(a_h[h].astype(BF16), vh) + _dot_nt(qi[:, h * DK:(h + 1) * DK], st.astype(BF16))
            st_ref[h] = st * dec[:, h * DK:(h + 1) * DK] + _dot_tn(vh, kd[:, h * DK:(h + 1) * DK])
            if final:
                tot = o + oth_ref[rows, h * DV:(h + 1) * DV]
                y = _rms(tot, g_ref[...]) * _silu(og_ref[rows, h * DV:(h + 1) * DV])
                outs.append(y.astype(o_ref.dtype))
            else:
                outs.append(o)
        o_ref[rows, :] = jnp.concatenate(outs, axis=-1)
        return carry

    lax.fori_loop(0, nchunks, chunk, 0)


def _gla_pass(pg, other, w_alpha, b_alpha, norm_g, *, rev, final):
    N, S, _ = pg.shape
    HK, HV = GLA_HEADS * GLA_DK, GLA_HEADS * GLA_DV
    tb = min(256, S)
    nb = S // tb
    blk = (lambda i: nb - 1 - i) if rev else (lambda i: i)
    tok = lambda w, j: pl.BlockSpec((None, tb, w), lambda n, i: (n, blk(i), j))
    const = lambda shape: pl.BlockSpec(shape, lambda n, i: (0,) * len(shape))
    mstack = jnp.asarray(_gla_tables(rev), BF16)
    d = 1 if rev else 0
    if other is None:
        other = jnp.zeros((N, S, HV), F32)
    out_dtype = BF16 if final else F32
    return pl.pallas_call(
        functools.partial(_gla_kernel, rev=rev, final=final, nchunks=tb // GLA_CHUNK),
        grid=(N, nb),
        in_specs=[tok(HK, 0), tok(HK, 1), tok(HV, 1), tok(128, (2 * HK + 2 * HV) // 128),
                  tok(HV, 2), tok(HV, 0),
                  const(mstack.shape), const((GLA_RANK, HK)), const((1, HK)), const((1, GLA_DV))],
        out_specs=pl.BlockSpec((None, tb, HV), lambda n, i: (n, blk(i), 0)),
        out_shape=jax.ShapeDtypeStruct((N, S, HV), out_dtype),
        scratch_shapes=[pltpu.VMEM((GLA_HEADS, GLA_DV, GLA_DK), F32)],
        compiler_params=_cparams(("arbitrary", "arbitrary")),
        name="gla_rev" if rev else "gla_fwd",
    )(pg, pg, pg, pg, pg, other, mstack, w_alpha[d], b_alpha[d].reshape(1, HK),
      norm_g.reshape(1, GLA_DV))


def _softplus(x):
    return jnp.maximum(x, 0.0) + jnp.log1p(jnp.exp(-jnp.abs(x)))


def _gelu_tanh(x):
    return 0.5 * x * (1.0 + jnp.tanh(math.sqrt(2.0 / math.pi) * (x + 0.044715 * (x * x * x))))


def _lru_kernel(x_ref, halo_ref, rg_ref, oth_ref, cw_ref, cb_ref, wa_ref, ba_ref, wx_ref, bx_ref,
                lam_ref, o_ref, ext_ref, car_ref, *, rev, final, tb):
    W = x_ref.shape[-1]
    i = pl.program_id(1)
    first = i == 0

    @pl.when(first)
    def _():
        car_ref[...] = jnp.zeros_like(car_ref)

    halo = jnp.where(first, 0.0, halo_ref[...])
    if rev:
        ext_ref[pl.ds(0, tb), :] = x_ref[...]
        ext_ref[pl.ds(tb, LRU_HALO), :] = halo
    else:
        ext_ref[pl.ds(0, LRU_HALO), :] = halo
        ext_ref[pl.ds(LRU_HALO, tb), :] = x_ref[...]
    cw = cw_ref[...]
    xc = jnp.zeros((tb, W), F32) + cb_ref[...]
    for tap in range(LRU_TAPS):
        off = (LRU_TAPS - 1 - tap) if rev else (LRU_HALO - (LRU_TAPS - 1) + tap)
        xc = xc + cw[tap:tap + 1, :] * ext_ref[pl.ds(off, tb), :]

    xcb = xc.astype(BF16)
    r = _sigmoid(_dot(xcb, wa_ref[...]) + ba_ref[...])
    g = _sigmoid(_dot(xcb, wx_ref[...]) + bx_ref[...])
    log_a = (-LRU_C) * r * _softplus(-lam_ref[...])
    a = jnp.exp(log_a)
    h = jnp.sqrt(1.0 - jnp.exp(2.0 * log_a)) * (g * xc)

    row = lax.broadcasted_iota(jnp.int32, (tb, W), 0)
    d = 1
    while d < tb:
        if rev:
            valid = row < tb - d
            shift = tb - d
        else:
            valid = row >= d
            shift = d
        a_s = jnp.where(valid, pltpu.roll(a, shift, 0), 1.0)
        h_s = jnp.where(valid, pltpu.roll(h, shift, 0), 0.0)
        h = h + a * h_s
        a = a * a_s
        d *= 2
    h = h + a * car_ref[...]
    car_ref[...] = h[0:1, :] if rev else h[tb - 1:tb, :]
    if final:
        o_ref[...] = ((h + oth_ref[...]) * _gelu_tanh(rg_ref[...])).astype(o_ref.dtype)
    else:
        o_ref[...] = h


def _lru_pass(pr, other, conv_w, conv_b, wa_bd, b_a, wx_bd, b_x, lam, *, rev, final):
    N, S, _ = pr.shape
    W = GROUP_WIDTH
    tb = min(256, S)
    nb = S // tb
    hb = tb // LRU_HALO
    blk = (lambda i: nb - 1 - i) if rev else (lambda i: i)
    if rev:
        halo_blk = lambda i: jnp.minimum((blk(i) + 1) * hb, S // LRU_HALO - 1)
    else:
        halo_blk = lambda i: jnp.maximum(blk(i) * hb - 1, 0)
    tok = lambda j: pl.BlockSpec((None, tb, W), lambda n, i: (n, blk(i), j))
    const = lambda shape: pl.BlockSpec(shape, lambda n, i: (0,) * len(shape))
    if other is None:
        other = jnp.zeros((N, S, W), F32)
    out_dtype = BF16 if final else F32
    d = 1 if rev else 0
    return pl.pallas_call(
        functools.partial(_lru_kernel, rev=rev, final=final, tb=tb),
        grid=(N, nb),
        in_specs=[tok(0),
                  pl.BlockSpec((None, LRU_HALO, W), lambda n, i: (n, halo_blk(i), 0)),
                  tok(1), tok(0),
                  const((LRU_TAPS, W)), const((1, W)), const((W, W)), const((1, W)),
                  const((W, W)), const((1, W)), const((1, W))],
        out_specs=pl.BlockSpec((None, tb, W), lambda n, i: (n, blk(i), 0)),
        out_shape=jax.ShapeDtypeStruct((N, S, W), out_dtype),
        scratch_shapes=[pltpu.VMEM((tb + LRU_HALO, W), F32), pltpu.VMEM((1, W), F32)],
        compiler_params=_cparams(("arbitrary", "arbitrary")),
        name="lru_rev" if rev else "lru_fwd",
    )(pr, pr, pr, other, conv_w[d], conv_b[d].reshape(1, W), wa_bd[d], b_a[d].reshape(1, W),
      wx_bd[d], b_x[d].reshape(1, W), lam[d].reshape(1, W))


def _conf_kernel(x_ref, prev_ref, next_ref, cw_ref, cb_ref, lg_ref, lb_ref, o_ref, ext_ref, *, tb):
    W = GROUP_WIDTH
    i = pl.program_id(1)
    last = pl.num_programs(1) - 1

    def glu(t):
        return t[:, :W] * _sigmoid(t[:, W:])

    ext_ref[pl.ds(0, CONF_HALO), :] = jnp.where(i == 0, 0.0, glu(prev_ref[...]))
    ext_ref[pl.ds(CONF_HALO, tb), :] = glu(x_ref[...])
    ext_ref[pl.ds(CONF_HALO + tb, CONF_HALO), :] = jnp.where(i == last, 0.0, glu(next_ref[...]))
    cw = cw_ref[...]
    u = jnp.zeros((tb, W), F32) + cb_ref[...]
    base = CONF_HALO - CONF_TAPS // 2
    for tap in range(CONF_TAPS):
        u = u + cw[tap:tap + 1, :] * ext_ref[pl.ds(base + tap, tb), :]
    uc = u - jnp.mean(u, axis=-1, keepdims=True)
    y = uc * lax.rsqrt(jnp.mean(uc * uc, axis=-1, keepdims=True) + EPS) * lg_ref[...] + lb_ref[...]
    o_ref[...] = _silu(y).astype(o_ref.dtype)


def _conformer(pc, dw_w, dw_b, ln_g, ln_b):
    N, S, W2 = pc.shape
    W = GROUP_WIDTH
    tb = min(256, S)
    nb = S // tb
    hb = tb // CONF_HALO
    taps_pad = 32
    cw = jnp.zeros((taps_pad, W), F32).at[:CONF_TAPS].set(dw_w)
    const = lambda shape: pl.BlockSpec(shape, lambda n, i: (0,) * len(shape))
    return pl.pallas_call(
        functools.partial(_conf_kernel, tb=tb),
        grid=(N, nb),
        in_specs=[pl.BlockSpec((None, tb, W2), lambda n, i: (n, i, 0)),
                  pl.BlockSpec((None, CONF_HALO, W2), lambda n, i: (n, jnp.maximum(i * hb - 1, 0), 0)),
                  pl.BlockSpec((None, CONF_HALO, W2),
                               lambda n, i: (n, jnp.minimum((i + 1) * hb, S // CONF_HALO - 1), 0)),
                  const((taps_pad, W)), const((1, W)), const((1, W)), const((1, W))],
        out_specs=pl.BlockSpec((None, tb, W), lambda n, i: (n, i, 0)),
        out_shape=jax.ShapeDtypeStruct((N, S, W), BF16),
        scratch_shapes=[pltpu.VMEM((tb + 2 * CONF_HALO, W), F32)],
        compiler_params=_cparams(("arbitrary", "arbitrary")),
        name="conformer",
    )(pc, pc, pc, cw, dw_b.reshape(1, W), ln_g.reshape(1, W), ln_b.reshape(1, W))


def _mix_out_kernel(x_ref, gt_ref, gpost_ref, yf_ref, yg_ref, yr_ref, yc_ref, w_ref, o_ref):
    W = GROUP_WIDTH
    acc = _dot(yf_ref[...], w_ref[pl.ds(0, W), :])
    acc += _dot(yg_ref[...], w_ref[pl.ds(W, W), :])
    acc += _dot(yr_ref[...], w_ref[pl.ds(2 * W, W), :])
    acc += _dot(yc_ref[...], w_ref[pl.ds(3 * W, W), :])
    o_ref[...] = x_ref[...] + gt_ref[...] * _rms(acc, gpost_ref[...])


def _mix_out(x, mod_l, g_post, y_f, y_g, y_r, y_c, w_out):
    N, S, D = x.shape
    W = GROUP_WIDTH
    tm = min(512, S)
    sub = 1
    tok = lambda w: pl.BlockSpec((None, tm, w), lambda n, i: (n, i, 0))
    return pl.pallas_call(
        _mix_out_kernel,
        grid=(N, S // tm),
        in_specs=[tok(D),
                  pl.BlockSpec((None, None, 1, D), lambda n, i: (n, 3 * sub + 2, 0, 0)),
                  pl.BlockSpec((1, D), lambda n, i: (0, 0)),
                  tok(W), tok(W), tok(W), tok(W),
                  pl.BlockSpec(w_out.shape, lambda n, i: (0, 0), pipeline_mode=pl.Buffered(1))],
        out_specs=tok(D),
        out_shape=jax.ShapeDtypeStruct((N, S, D), F32),
        compiler_params=_cparams(("arbitrary", "arbitrary")),
        name="mix_out",
    )(x, mod_l, g_post.reshape(1, D), y_f, y_g, y_r, y_c, w_out)


def _dft_tables(S):
    c = np.arange(FNET_GROUP_DIM)
    ang = 2.0 * np.pi * np.outer(c, c) / FNET_GROUP_DIM
    eye = np.eye(FNET_GROUPS)
    scale = FNET_GROUP_DIM ** -0.5
    cs128 = np.concatenate([np.kron(eye, np.cos(ang) * scale), np.kron(eye, np.sin(ang) * scale)], axis=1)
    t = jnp.arange(S, dtype=jnp.int32)
    k = (t[:, None] * t[None, :]) % S
    ang_s = k.astype(F32) * (2.0 * math.pi / S)
    table = jnp.concatenate([jnp.cos(ang_s), -jnp.sin(ang_s)], axis=1) * (S ** -0.5)
    return jnp.asarray(cs128, BF16), table.astype(BF16)


def _block_diag(w):
    nd, nblk, bd, _ = w.shape
    out = jnp.zeros((nd, nblk * bd, nblk * bd), w.dtype)
    for b in range(nblk):
        out = out.at[:, b * bd:(b + 1) * bd, b * bd:(b + 1) * bd].set(w[:, b])
    return out


def kernel(x_prompt, x_sample, c_prompt, c_sample, w_ada, b_ada, g_pre, g_post, ffn1_w_in, ffn1_w_out, ffn2_w_in, ffn2_w_out, w_mix_in, gla_w_alpha, gla_b_alpha, gla_norm_g, lru_conv_w, lru_conv_b, lru_w_a, lru_b_a, lru_w_x, lru_b_x, lru_lambda, conf_dw_w, conf_dw_b, conf_ln_g, conf_ln_b, w_mix_out):
    assert x_prompt.shape[1:] == x_sample.shape[1:]
    B0 = x_prompt.shape[0]
    x = jnp.concatenate([x_prompt, x_sample], axis=0)
    N, S, D = x.shape
    L = w_ada.shape[0]
    NP = -(-N // 8) * 8
    c = jnp.zeros((NP, D), F32).at[:N].set(jnp.concatenate([c_prompt, c_sample], axis=0))

    mod = _adaln_mod(c, w_ada, b_ada).reshape(L, NP, 3 * N_SUB, 1, D)
    cs128, table = _dft_tables(S)

    W = GROUP_WIDTH
    HK, HV = GLA_HEADS * GLA_DK, GLA_HEADS * GLA_DV
    o_q = W
    o_k = o_q + HK
    o_v = o_k + HK
    o_og = o_v + HV
    o_alr = o_og + HV
    o_rin = o_alr + 2 * GLA_RANK
    o_rg = o_rin + W
    o_c = o_rg + W

    for l in range(L):
        wmi = w_mix_in[l]
        wf = wmi[:, :W].astype(BF16)
        wg = jnp.concatenate([wmi[:, o_q:o_alr], wmi[:, o_alr:o_rin],
                              jnp.zeros((D, 128 - 2 * GLA_RANK), F32)], axis=1).astype(BF16)
        wr = wmi[:, o_rin:o_c].astype(BF16)
        wc = wmi[:, o_c:].astype(BF16)

        x = _ffn(x, mod[l], 0, g_pre[l, 0], g_post[l, 0],
                 ffn1_w_in[l].astype(BF16), ffn1_w_out[l].astype(BF16), 0.5)

        xcs, pg, pr, pc = _mix_in(x, mod[l], g_pre[l, 1], wf, wg, wr, wc, cs128)
        y_f = _seq_dft(xcs, table)

        o_rev = _gla_pass(pg, None, gla_w_alpha[l], gla_b_alpha[l], gla_norm_g[l], rev=True, final=False)
        y_g = _gla_pass(pg, o_rev, gla_w_alpha[l], gla_b_alpha[l], gla_norm_g[l], rev=False, final=True)

        wa_bd = _block_diag(lru_w_a[l]).astype(BF16)
        wx_bd = _block_diag(lru_w_x[l]).astype(BF16)
        lru_args = (lru_conv_w[l], lru_conv_b[l], wa_bd, lru_b_a[l], wx_bd, lru_b_x[l], lru_lambda[l])
        h_rev = _lru_pass(pr, None, *lru_args, rev=True, final=False)
        y_r = _lru_pass(pr, h_rev, *lru_args, rev=False, final=True)

        y_c = _conformer(pc, conf_dw_w[l], conf_dw_b[l], conf_ln_g[l], conf_ln_b[l])

        x = _mix_out(x, mod[l], g_post[l, 1], y_f, y_g, y_r, y_c, w_mix_out[l].astype(BF16))

        x = _ffn(x, mod[l], 2, g_pre[l, 2], g_post[l, 2],
                 ffn2_w_in[l].astype(BF16), ffn2_w_out[l].astype(BF16), 0.5)

    return (x[:B0], x[B0:])
```

```python
import functools
import math

import numpy as np
import jax
import jax.numpy as jnp
from jax import lax
from jax.experimental import pallas as pl
from jax.experimental.pallas import tpu as pltpu

F32 = jnp.float32
BF16 = jnp.bfloat16

EPS = 1e-6
N_SUB = 3
GROUP_WIDTH = 512
FNET_GROUPS = 4
FNET_GROUP_DIM = 128
GLA_HEADS = 4
GLA_DK = 64
GLA_DV = 128
GLA_RANK = 16
GLA_TAU = 16.0
GLA_CHUNK = 64
LRU_BLOCKS = 8
LRU_BLOCK_DIM = 64
LRU_C = 8.0
LRU_TAPS = 4
CONF_TAPS = 31
CONF_HALO = 16
LRU_HALO = 8

V7X_VMEM_BYTES = 64 * 1024 * 1024
VMEM_LIMIT = 52 * 1024 * 1024

NEG_BIG = -1e30


def _cparams(sem):
    return pltpu.CompilerParams(dimension_semantics=sem, vmem_limit_bytes=VMEM_LIMIT)


def _sigmoid(x):
    return 1.0 / (1.0 + jnp.exp(-x))


def _silu(x):
    return x * _sigmoid(x)


def _rms(x, g):
    return x * lax.rsqrt(jnp.mean(x * x, axis=-1, keepdims=True) + EPS) * g


def _dot(a, b):
    return jnp.dot(a, b, preferred_element_type=F32)


def _dot_nt(a, b):
    return lax.dot_general(a, b, (((1,), (1,)), ((), ())), preferred_element_type=F32)


def _dot_tn(a, b):
    return lax.dot_general(a, b, (((0,), (0,)), ((), ())), preferred_element_type=F32)


def _mod_kernel(c_ref, w_ref, b_ref, o_ref):
    c = c_ref[...]
    o_ref[...] = jnp.dot(_silu(c), w_ref[...], preferred_element_type=F32,
                         precision=lax.Precision.HIGHEST) + b_ref[...]


def _adaln_mod(c_pad, w_ada, b_ada):
    L, D, W = w_ada.shape
    NP = c_pad.shape[0]
    tn = 1024
    return pl.pallas_call(
        _mod_kernel,
        grid=(L, W // tn),
        in_specs=[
            pl.BlockSpec((NP, D), lambda l, j: (0, 0)),
            pl.BlockSpec((None, D, tn), lambda l, j: (l, 0, j)),
            pl.BlockSpec((None, 1, tn), lambda l, j: (l, 0, j)),
        ],
        out_specs=pl.BlockSpec((None, NP, tn), lambda l, j: (l, 0, j)),
        out_shape=jax.ShapeDtypeStruct((L, NP, W), F32),
        compiler_params=_cparams(("arbitrary", "arbitrary")),
        name="adaln_mod",
    )(c_pad, w_ada, b_ada.reshape(L, 1, W))


def _ffn_kernel(*refs, res_w, n_in, n_out, n_split):
    x_refs = refs[:n_in]
    sh_ref, sc_ref, gt_ref, gpre_ref, gpost_ref, wu_ref, wg_ref, wo_ref = refs[n_in:n_in + 8]
    o_refs = refs[n_in + 8:n_in + 8 + n_out]
    xn_ref, acc_ref = refs[n_in + 8 + n_out:]
    n = pl.program_id(0)
    f = pl.program_id(2)
    is_first, is_last = f == 0, f == pl.num_programs(2) - 1
    in_second = n >= n_split

    def prologue(x_ref):
        x = x_ref[...]
        xn = _rms(x, gpre_ref[...]) * (1.0 + sc_ref[...]) + sh_ref[...]
        xn_ref[...] = xn.astype(BF16)
        acc_ref[...] = jnp.zeros_like(acc_ref)

    if n_in == 1:
        pl.when(is_first)(lambda: prologue(x_refs[0]))
    else:
        pl.when(is_first & jnp.logical_not(in_second))(lambda: prologue(x_refs[0]))
        pl.when(is_first & in_second)(lambda: prologue(x_refs[1]))

    xn = xn_ref[...]
    up = _dot(xn, wu_ref[...])
    gate = _dot(xn, wg_ref[...])
    act = (_silu(gate) * up).astype(BF16)
    acc_ref[...] += _dot(act, wo_ref[...])

    def epilogue(x_ref, o_ref):
        o = acc_ref[...]
        o_ref[...] = x_ref[...] + (res_w * gt_ref[...]) * _rms(o, gpost_ref[...])

    if n_in == 1 and n_out == 1:
        pl.when(is_last)(lambda: epilogue(x_refs[0], o_refs[0]))
    else:
        pl.when(is_last & jnp.logical_not(in_second))(lambda: epilogue(x_refs[0], o_refs[0]))
        pl.when(is_last & in_second)(lambda: epilogue(x_refs[-1], o_refs[-1]))


def _ffn(xs, mod_l, sub, g_pre, g_post, w_in, w_out, res_w, split_out=None):
    xs = xs if isinstance(xs, (tuple, list)) else (xs,)
    S, D = xs[0].shape[1:]
    N = sum(a.shape[0] for a in xs)
    n_split = xs[0].shape[0] if len(xs) == 2 else (split_out if split_out else N)
    FF = w_out.shape[0]
    tm = min(512, S)
    tf = 512
    nf = FF // tf
    ni = S // tm

    def first_map(n, i, f):
        return (jnp.minimum(n, n_split - 1), jnp.where(n < n_split, i, ni - 1), 0)

    def second_map(n, i, f):
        return (jnp.maximum(n - n_split, 0), jnp.where(n >= n_split, i, 0), 0)

    whole_map = lambda n, i, f: (n, i, 0)
    tok = lambda imap: pl.BlockSpec((None, tm, D), imap)
    x_specs = [tok(whole_map)] if len(xs) == 1 else [tok(first_map), tok(second_map)]
    if split_out:
        o_specs = [tok(first_map), tok(second_map)]
        o_shapes = [jax.ShapeDtypeStruct((n_split, S, D), F32), jax.ShapeDtypeStruct((N - n_split, S, D), F32)]
    else:
        o_specs = [tok(whole_map)]
        o_shapes = [jax.ShapeDtypeStruct((N, S, D), F32)]
    mspec = lambda r: pl.BlockSpec((None, None, 1, D), lambda n, i, f: (n, 3 * sub + r, 0, 0))
    out = pl.pallas_call(
        functools.partial(_ffn_kernel, res_w=res_w, n_in=len(xs), n_out=len(o_specs), n_split=n_split),
        grid=(N, ni, nf),
        in_specs=x_specs + [
            mspec(0), mspec(1), mspec(2),
            pl.BlockSpec((1, D), lambda n, i, f: (0, 0)),
            pl.BlockSpec((1, D), lambda n, i, f: (0, 0)),
            pl.BlockSpec((D, tf), lambda n, i, f: (0, f)),
            pl.BlockSpec((D, tf), lambda n, i, f: (0, f + nf)),
            pl.BlockSpec((tf, D), lambda n, i, f: (f, 0)),
        ],
        out_specs=o_specs,
        out_shape=o_shapes,
        scratch_shapes=[pltpu.VMEM((tm, D), BF16), pltpu.VMEM((tm, D), F32)],
        compiler_params=_cparams(("arbitrary", "arbitrary", "arbitrary")),
        name="ffn",
    )(*xs, mod_l, mod_l, mod_l, g_pre.reshape(1, D), g_post.reshape(1, D), w_in, w_in, w_out)
    return tuple(out) if split_out else out[0]


def _mix_in_kernel(x_ref, sh_ref, sc_ref, gpre_ref, wf_ref, wg_ref, wr_ref, wc_ref,
                   pf_ref, pg_ref, pr_ref, pc_ref):
    x = x_ref[...]
    xn = (_rms(x, gpre_ref[...]) * (1.0 + sc_ref[...]) + sh_ref[...]).astype(BF16)
    pf_ref[...] = _dot(xn, wf_ref[...])
    pg_ref[...] = _dot(xn, wg_ref[...])
    pr_ref[...] = _dot(xn, wr_ref[...])
    pc_ref[...] = _dot(xn, wc_ref[...])


def _mix_in(x, mod_l, g_pre, wf, wg, wr, wc):
    N, S, D = x.shape
    tm = min(256, S)
    sub = 1
    mspec = lambda r: pl.BlockSpec((None, None, 1, D), lambda n, i: (n, 3 * sub + r, 0, 0))
    const = lambda shape: pl.BlockSpec(shape, lambda n, i: (0, 0), pipeline_mode=pl.Buffered(1))
    tok = lambda w: pl.BlockSpec((None, tm, w), lambda n, i: (n, i, 0))
    widths = [w.shape[1] for w in (wf, wg, wr, wc)]
    return pl.pallas_call(
        _mix_in_kernel,
        grid=(N, S // tm),
        in_specs=[tok(D), mspec(0), mspec(1), const((1, D)),
                  const(wf.shape), const(wg.shape), const(wr.shape), const(wc.shape)],
        out_specs=[tok(w) for w in widths],
        out_shape=[jax.ShapeDtypeStruct((N, S, w), F32) for w in widths],
        compiler_params=_cparams(("arbitrary", "arbitrary")),
        name="mix_in",
    )(x, mod_l, mod_l, g_pre.reshape(1, D), wf, wg, wr, wc)


DFT_B = 128
DFT_PAD = 8
DFT_UNROLL = 4


def _fourier_kernel(x_ref, t1_ref, m2_ref, cs_ref, o_ref, vs_ref, ys_ref, *, A):
    B = DFT_B
    pv = A + DFT_PAD
    py = B + DFT_PAD
    GW = FNET_GROUP_DIM

    def stage1(t1, carry):
        xs = x_ref[pl.ds(t1, B, stride=A), :].astype(BF16)
        v = _dot(t1_ref[t1], xs)
        vs_ref[0, pl.ds(t1, B, stride=pv), :] = v[:B]
        vs_ref[1, pl.ds(t1, B, stride=pv), :] = v[B:]
        return carry

    lax.fori_loop(0, A, stage1, 0, unroll=DFT_UNROLL)
    m2 = m2_ref[...]

    def stage2(f2, carry):
        r0 = pl.multiple_of(f2 * pv, 8)
        r = jnp.concatenate([vs_ref[0, pl.ds(r0, A), :], vs_ref[1, pl.ds(r0, A), :]], axis=0)
        y = _dot(m2, r.astype(BF16))
        ys_ref[0, pl.ds(f2, A, stride=py), :] = y[:A]
        ys_ref[1, pl.ds(f2, A, stride=py), :] = y[A:]
        return carry

    lax.fori_loop(0, B, stage2, 0, unroll=DFT_UNROLL)
    c128 = cs_ref[pl.ds(0, GW), :]
    s128 = cs_ref[pl.ds(GW, GW), :]

    def stage3(f1, carry):
        r0 = pl.multiple_of(f1 * py, 8)
        out = _dot(ys_ref[0, pl.ds(r0, B), :].astype(BF16), c128)
        out += _dot(ys_ref[1, pl.ds(r0, B), :].astype(BF16), s128)
        o_ref[pl.ds(pl.multiple_of(f1 * B, B), B), :] = out.astype(o_ref.dtype)
        return carry

    lax.fori_loop(0, A, stage3, 0, unroll=DFT_UNROLL)


def _fourier_tables(S):
    B = DFT_B
    A = S // B
    t1 = np.arange(A)[:, None, None]
    f2 = np.arange(B)[None, :, None]
    t2 = np.arange(B)[None, None, :]
    ang = 2.0 * np.pi * ((f2 * (t1 + A * t2)) % S) / S
    tab1 = np.concatenate([np.cos(ang), -np.sin(ang)], axis=1)
    a = np.arange(A)
    ang_a = 2.0 * np.pi * ((a[:, None] * a[None, :]) % A) / A
    ca, sa = np.cos(ang_a), np.sin(ang_a)
    m2 = np.block([[ca, sa], [-sa, ca]])
    c = np.arange(FNET_GROUP_DIM)
    ang_c = 2.0 * np.pi * ((c[:, None] * c[None, :]) % FNET_GROUP_DIM) / FNET_GROUP_DIM
    scale = (S * FNET_GROUP_DIM) ** -0.5
    cs = np.concatenate([np.cos(ang_c), np.sin(ang_c)], axis=0) * scale
    return jnp.asarray(tab1, BF16), jnp.asarray(m2, BF16), jnp.asarray(cs, BF16)


def _fourier(pf, tables):
    N, S, W = pf.shape
    GW = FNET_GROUP_DIM
    B = DFT_B
    A = S // B
    tab1, m2, cs = tables
    const = lambda a: pl.BlockSpec(a.shape, lambda n, g: (0,) * a.ndim, pipeline_mode=pl.Buffered(1))
    return pl.pallas_call(
        functools.partial(_fourier_kernel, A=A),
        grid=(N, W // GW),
        in_specs=[pl.BlockSpec((None, S, GW), lambda n, g: (n, 0, g)), const(tab1), const(m2), const(cs)],
        out_specs=pl.BlockSpec((None, S, GW), lambda n, g: (n, 0, g)),
        out_shape=jax.ShapeDtypeStruct((N, S, W), BF16),
        scratch_shapes=[pltpu.VMEM((2, B * (A + DFT_PAD), GW), F32),
                        pltpu.VMEM((2, A * (B + DFT_PAD), GW), F32)],
        compiler_params=_cparams(("arbitrary", "arbitrary")),
        name="fourier",
    )(pf, tab1, m2, cs)


def _gla_tables(rev):
    C = GLA_CHUNK
    idx = np.arange(C)
    if not rev:
        cum = (idx[None, :] <= idx[:, None]).astype(np.float32)
    else:
        cum = (idx[None, :] >= idx[:, None]).astype(np.float32)
    mats = []
    m = C // 2
    while m >= 1:
        base = (idx // (2 * m)) * (2 * m)
        piv = base + (m if rev else m - 1)
        mats.append(cum - cum[piv, :])
        m //= 2
    mats.append(cum)
    return np.concatenate(mats, axis=0)


def _split3(x):
    hi = x.astype(BF16)
    r1 = x - hi.astype(F32)
    mid = r1.astype(BF16)
    lo = (r1 - mid.astype(F32)).astype(BF16)
    return hi, mid, lo


def _gla_kernel(*refs, rev, final, nchunks):
    q_ref, k_ref, v_ref, alr_ref = refs[:4]
    og_ref, oth_ref = refs[4:6] if final else (None, None)
    m_ref, wa_ref, ba_ref, g_ref, o_ref, st_ref = refs[-6:]
    C, H, DK, DV = GLA_CHUNK, GLA_HEADS, GLA_DK, GLA_DV
    HK = H * DK

    @pl.when(pl.program_id(1) == 0)
    def _():
        st_ref[...] = jnp.zeros_like(st_ref)

    row = lax.broadcasted_iota(jnp.int32, (C, HK), 0)
    t_i = lax.broadcasted_iota(jnp.int32, (C, C), 0)
    s_i = lax.broadcasted_iota(jnp.int32, (C, C), 1)
    mstack = m_ref[...]
    d0 = (GLA_RANK if rev else 0)

    def chunk(ci, carry):
        cidx = (nchunks - 1 - ci) if rev else ci
        r0 = pl.multiple_of(cidx * C, C)
        rows = pl.ds(r0, C)
        q = q_ref[rows, :] * (DK ** -0.5)
        k = k_ref[rows, :]
        v = v_ref[rows, :].astype(BF16)
        a16 = alr_ref[rows, :][:, d0:d0 + GLA_RANK]
        z = jnp.dot(a16, wa_ref[...], preferred_element_type=F32,
                    precision=lax.Precision.HIGHEST) + ba_ref[...]
        la = (jnp.minimum(z, 0.0) - jnp.log1p(jnp.exp(-jnp.abs(z)))) / GLA_TAU
        hi, mid, lo = _split3(la)
        e_all = _dot(mstack, hi) + _dot(mstack, mid) + _dot(mstack, lo)

        a_h = [jnp.zeros((C, C), F32) for _ in range(H)]
        m = C // 2
        lvl = 0
        while m >= 1:
            d = e_all[lvl * C:(lvl + 1) * C]
            upper = (row & (2 * m - 1)) >= m
            q_act, k_act = (jnp.logical_not(upper), upper) if rev else (upper, jnp.logical_not(upper))
            qh = (q * jnp.exp(jnp.where(q_act, d, NEG_BIG))).astype(BF16)
            kh = (k * jnp.exp(jnp.where(k_act, -d, NEG_BIG))).astype(BF16)
            sh = int(math.log2(2 * m))
            same = (t_i >> sh) == (s_i >> sh)
            for h in range(H):
                p = _dot_nt(qh[:, h * DK:(h + 1) * DK], kh[:, h * DK:(h + 1) * DK])
                a_h[h] = a_h[h] + jnp.where(same, p, 0.0)
            m //= 2
            lvl += 1
        qb, kb = q.astype(BF16), k.astype(BF16)
        diag = t_i == s_i
        for h in range(H):
            p = _dot_nt(qb[:, h * DK:(h + 1) * DK], kb[:, h * DK:(h + 1) * DK])
            a_h[h] = a_h[h] + jnp.where(diag, p, 0.0)

        b = e_all[6 * C:7 * C]
        b_edge = b[0:1, :] if rev else b[C - 1:C, :]
        qi = (q * jnp.exp(b)).astype(BF16)
        kd = (k * jnp.exp(b_edge - b)).astype(BF16)
        dec = jnp.exp(b_edge)
        outs = []
        for h in range(H):
            st = st_ref[h]
            vh = v[:, h * DV:(h + 1) * DV]
            o = _dot(a_h[h].astype(BF16), vh) + _dot_nt(qi[:, h * DK:(h + 1) * DK], st.astype(BF16))
            st_ref[h] = st * dec[:, h * DK:(h + 1) * DK] + _dot_tn(vh, kd[:, h * DK:(h + 1) * DK])
            if final:
                tot = o + oth_ref[rows, h * DV:(h + 1) * DV]
                y = _rms(tot, g_ref[...]) * _silu(og_ref[rows, h * DV:(h + 1) * DV])
                outs.append(y.astype(o_ref.dtype))
            else:
                outs.append(o)
        o_ref[rows, :] = jnp.concatenate(outs, axis=-1)
        return carry

    lax.fori_loop(0, nchunks, chunk, 0, unroll=2)


def _gla_pass(pg, other, w_alpha, b_alpha, norm_g, *, rev, final):
    N, S, _ = pg.shape
    HK, HV = GLA_HEADS * GLA_DK, GLA_HEADS * GLA_DV
    tb = min(256, S)
    nb = S // tb
    blk = (lambda i: nb - 1 - i) if rev else (lambda i: i)
    tok = lambda w, j: pl.BlockSpec((None, tb, w), lambda n, i: (n, blk(i), j))
    const = lambda shape: pl.BlockSpec(shape, lambda n, i: (0,) * len(shape))
    mstack = jnp.asarray(_gla_tables(rev), BF16)
    d = 1 if rev else 0
    out_dtype = BF16 if final else F32
    extra_specs = [tok(HV, 2), tok(HV, 0)] if final else []
    extra_args = (pg, other) if final else ()
    return pl.pallas_call(
        functools.partial(_gla_kernel, rev=rev, final=final, nchunks=tb // GLA_CHUNK),
        grid=(N, nb),
        in_specs=[tok(HK, 0), tok(HK, 1), tok(HV, 1), tok(128, (2 * HK + 2 * HV) // 128)]
                 + extra_specs
                 + [const(mstack.shape), const((GLA_RANK, HK)), const((1, HK)), const((1, GLA_DV))],
        out_specs=pl.BlockSpec((None, tb, HV), lambda n, i: (n, blk(i), 0)),
        out_shape=jax.ShapeDtypeStruct((N, S, HV), out_dtype),
        scratch_shapes=[pltpu.VMEM((GLA_HEADS, GLA_DV, GLA_DK), F32)],
        compiler_params=_cparams(("arbitrary", "arbitrary")),
        name="gla_rev" if rev else "gla_fwd",
    )(pg, pg, pg, pg, *extra_args, mstack, w_alpha[d], b_alpha[d].reshape(1, HK),
      norm_g.reshape(1, GLA_DV))


def _softplus(x):
    return jnp.maximum(x, 0.0) + jnp.log1p(jnp.exp(-jnp.abs(x)))


def _gelu_tanh(x):
    return 0.5 * x * (1.0 + jnp.tanh(math.sqrt(2.0 / math.pi) * (x + 0.044715 * (x * x * x))))


def _lru_kernel(*refs, rev, final, tb):
    x_ref, halo_ref = refs[:2]
    rg_ref, oth_ref = refs[2:4] if final else (None, None)
    cw_ref, cb_ref, wa_ref, ba_ref, wx_ref, bx_ref, lam_ref, o_ref, ext_ref, car_ref = refs[-10:]
    W = x_ref.shape[-1]
    i = pl.program_id(1)
    first = i == 0

    @pl.when(first)
    def _():
        car_ref[...] = jnp.zeros_like(car_ref)

    halo = jnp.where(first, 0.0, halo_ref[...])
    if rev:
        ext_ref[pl.ds(0, tb), :] = x_ref[...]
        ext_ref[pl.ds(tb, LRU_HALO), :] = halo
    else:
        ext_ref[pl.ds(0, LRU_HALO), :] = halo
        ext_ref[pl.ds(LRU_HALO, tb), :] = x_ref[...]
    cw = cw_ref[...]
    xc = jnp.zeros((tb, W), F32) + cb_ref[...]
    for tap in range(LRU_TAPS):
        off = (LRU_TAPS - 1 - tap) if rev else (LRU_HALO - (LRU_TAPS - 1) + tap)
        xc = xc + cw[tap:tap + 1, :] * ext_ref[pl.ds(off, tb), :]

    xcb = xc.astype(BF16)
    r = _sigmoid(_dot(xcb, wa_ref[...]) + ba_ref[...])
    g = _sigmoid(_dot(xcb, wx_ref[...]) + bx_ref[...])
    log_a = (-LRU_C) * r * _softplus(-lam_ref[...])
    a = jnp.exp(log_a)
    h = jnp.sqrt(1.0 - jnp.exp(2.0 * log_a)) * (g * xc)

    row = lax.broadcasted_iota(jnp.int32, (tb, W), 0)
    d = 1
    while d < tb:
        if rev:
            valid = row < tb - d
            shift = tb - d
        else:
            valid = row >= d
            shift = d
        a_s = jnp.where(valid, pltpu.roll(a, shift, 0), 1.0)
        h_s = jnp.where(valid, pltpu.roll(h, shift, 0), 0.0)
        h = h + a * h_s
        a = a * a_s
        d *= 2
    h = h + a * car_ref[...]
    car_ref[...] = h[0:1, :] if rev else h[tb - 1:tb, :]
    if final:
        o_ref[...] = ((h + oth_ref[...]) * _gelu_tanh(rg_ref[...])).astype(o_ref.dtype)
    else:
        o_ref[...] = h


def _lru_pass(pr, other, conv_w, conv_b, wa_bd, b_a, wx_bd, b_x, lam, *, rev, final):
    N, S, _ = pr.shape
    W = GROUP_WIDTH
    tb = min(256, S)
    nb = S // tb
    hb = tb // LRU_HALO
    blk = (lambda i: nb - 1 - i) if rev else (lambda i: i)
    if rev:
        halo_blk = lambda i: jnp.minimum((blk(i) + 1) * hb, S // LRU_HALO - 1)
    else:
        halo_blk = lambda i: jnp.maximum(blk(i) * hb - 1, 0)
    tok = lambda j: pl.BlockSpec((None, tb, W), lambda n, i: (n, blk(i), j))
    const = lambda shape: pl.BlockSpec(shape, lambda n, i: (0,) * len(shape))
    out_dtype = BF16 if final else F32
    d = 1 if rev else 0
    extra_specs = [tok(1), tok(0)] if final else []
    extra_args = (pr, other) if final else ()
    return pl.pallas_call(
        functools.partial(_lru_kernel, rev=rev, final=final, tb=tb),
        grid=(N, nb),
        in_specs=[tok(0),
                  pl.BlockSpec((None, LRU_HALO, W), lambda n, i: (n, halo_blk(i), 0))]
                 + extra_specs
                 + [const((LRU_TAPS, W)), const((1, W)), const((W, W)), const((1, W)),
                    const((W, W)), const((1, W)), const((1, W))],
        out_specs=pl.BlockSpec((None, tb, W), lambda n, i: (n, blk(i), 0)),
        out_shape=jax.ShapeDtypeStruct((N, S, W), out_dtype),
        scratch_shapes=[pltpu.VMEM((tb + LRU_HALO, W), F32), pltpu.VMEM((1, W), F32)],
        compiler_params=_cparams(("arbitrary", "arbitrary")),
        name="lru_rev" if rev else "lru_fwd",
    )(pr, pr, *extra_args, conv_w[d], conv_b[d].reshape(1, W), wa_bd[d], b_a[d].reshape(1, W),
      wx_bd[d], b_x[d].reshape(1, W), lam[d].reshape(1, W))


def _conf_kernel(x_ref, prev_ref, next_ref, cw_ref, cb_ref, lg_ref, lb_ref, o_ref, ext_ref, sh_ref, *, tb):
    W = GROUP_WIDTH
    i = pl.program_id(1)
    last = pl.num_programs(1) - 1

    def glu(t):
        return t[:, :W] * _sigmoid(t[:, W:])

    ext_ref[pl.ds(0, CONF_HALO), :] = jnp.where(i == 0, 0.0, glu(prev_ref[...]))
    ext_ref[pl.ds(CONF_HALO, tb), :] = glu(x_ref[...])
    ext_ref[pl.ds(CONF_HALO + tb, CONF_HALO), :] = jnp.where(i == last, 0.0, glu(next_ref[...]))
    span = tb + 2 * CONF_HALO - 8
    for r in range(1, 8):
        sh_ref[r - 1] = ext_ref[pl.ds(r, span), :]
    cw = cw_ref[...]
    u = jnp.zeros((tb, W), F32) + cb_ref[...]
    base = CONF_HALO - CONF_TAPS // 2
    for tap in range(CONF_TAPS):
        q, r = divmod(base + tap, 8)
        src = ext_ref[pl.ds(8 * q, tb), :] if r == 0 else sh_ref[r - 1, pl.ds(8 * q, tb), :]
        u = u + cw[tap:tap + 1, :] * src
    uc = u - jnp.mean(u, axis=-1, keepdims=True)
    y = uc * lax.rsqrt(jnp.mean(uc * uc, axis=-1, keepdims=True) + EPS) * lg_ref[...] + lb_ref[...]
    o_ref[...] = _silu(y).astype(o_ref.dtype)


def _conformer(pc, dw_w, dw_b, ln_g, ln_b):
    N, S, W2 = pc.shape
    W = GROUP_WIDTH
    tb = min(256, S)
    nb = S // tb
    hb = tb // CONF_HALO
    taps_pad = 32
    cw = jnp.zeros((taps_pad, W), F32).at[:CONF_TAPS].set(dw_w)
    const = lambda shape: pl.BlockSpec(shape, lambda n, i: (0,) * len(shape))
    return pl.pallas_call(
        functools.partial(_conf_kernel, tb=tb),
        grid=(N, nb),
        in_specs=[pl.BlockSpec((None, tb, W2), lambda n, i: (n, i, 0)),
                  pl.BlockSpec((None, CONF_HALO, W2), lambda n, i: (n, jnp.maximum(i * hb - 1, 0), 0)),
                  pl.BlockSpec((None, CONF_HALO, W2),
                               lambda n, i: (n, jnp.minimum((i + 1) * hb, S // CONF_HALO - 1), 0)),
                  const((taps_pad, W)), const((1, W)), const((1, W)), const((1, W))],
        out_specs=pl.BlockSpec((None, tb, W), lambda n, i: (n, i, 0)),
        out_shape=jax.ShapeDtypeStruct((N, S, W), BF16),
        scratch_shapes=[pltpu.VMEM((tb + 2 * CONF_HALO, W), F32),
                        pltpu.VMEM((7, tb + 2 * CONF_HALO - 8, W), F32)],
        compiler_params=_cparams(("arbitrary", "arbitrary")),
        name="conformer",
    )(pc, pc, pc, cw, dw_b.reshape(1, W), ln_g.reshape(1, W), ln_b.reshape(1, W))


def _mix_out_kernel(x_ref, gt_ref, gpost_ref, yf_ref, yg_ref, yr_ref, yc_ref, w_ref, o_ref):
    W = GROUP_WIDTH
    acc = _dot(yf_ref[...], w_ref[pl.ds(0, W), :])
    acc += _dot(yg_ref[...], w_ref[pl.ds(W, W), :])
    acc += _dot(yr_ref[...], w_ref[pl.ds(2 * W, W), :])
    acc += _dot(yc_ref[...], w_ref[pl.ds(3 * W, W), :])
    o_ref[...] = x_ref[...] + gt_ref[...] * _rms(acc, gpost_ref[...])


def _mix_out(x, mod_l, g_post, y_f, y_g, y_r, y_c, w_out):
    N, S, D = x.shape
    W = GROUP_WIDTH
    tm = min(512, S)
    sub = 1
    tok = lambda w: pl.BlockSpec((None, tm, w), lambda n, i: (n, i, 0))
    return pl.pallas_call(
        _mix_out_kernel,
        grid=(N, S // tm),
        in_specs=[tok(D),
                  pl.BlockSpec((None, None, 1, D), lambda n, i: (n, 3 * sub + 2, 0, 0)),
                  pl.BlockSpec((1, D), lambda n, i: (0, 0)),
                  tok(W), tok(W), tok(W), tok(W),
                  pl.BlockSpec(w_out.shape, lambda n, i: (0, 0), pipeline_mode=pl.Buffered(1))],
        out_specs=tok(D),
        out_shape=jax.ShapeDtypeStruct((N, S, D), F32),
        compiler_params=_cparams(("arbitrary", "arbitrary")),
        name="mix_out",
    )(x, mod_l, g_post.reshape(1, D), y_f, y_g, y_r, y_c, w_out)


def _block_diag(w):
    nd, nblk, bd, _ = w.shape
    out = jnp.zeros((nd, nblk * bd, nblk * bd), w.dtype)
    for b in range(nblk):
        out = out.at[:, b * bd:(b + 1) * bd, b * bd:(b + 1) * bd].set(w[:, b])
    return out


def kernel(x_prompt, x_sample, c_prompt, c_sample, w_ada, b_ada, g_pre, g_post, ffn1_w_in, ffn1_w_out, ffn2_w_in, ffn2_w_out, w_mix_in, gla_w_alpha, gla_b_alpha, gla_norm_g, lru_conv_w, lru_conv_b, lru_w_a, lru_b_a, lru_w_x, lru_b_x, lru_lambda, conf_dw_w, conf_dw_b, conf_ln_g, conf_ln_b, w_mix_out):
    assert x_prompt.shape[1:] == x_sample.shape[1:]
    B0 = x_prompt.shape[0]
    N = B0 + x_sample.shape[0]
    S, D = x_prompt.shape[1:]
    L = w_ada.shape[0]
    NP = -(-N // 8) * 8
    c = jnp.zeros((NP, D), F32).at[:N].set(jnp.concatenate([c_prompt, c_sample], axis=0))

    mod = _adaln_mod(c, w_ada, b_ada).reshape(L, NP, 3 * N_SUB, 1, D)
    dft_tables = _fourier_tables(S)
    x = (x_prompt, x_sample)

    W = GROUP_WIDTH
    HK, HV = GLA_HEADS * GLA_DK, GLA_HEADS * GLA_DV
    o_q = W
    o_k = o_q + HK
    o_v = o_k + HK
    o_og = o_v + HV
    o_alr = o_og + HV
    o_rin = o_alr + 2 * GLA_RANK
    o_rg = o_rin + W
    o_c = o_rg + W

    for l in range(L):
        wmi = w_mix_in[l]
        wf = wmi[:, :W].astype(BF16)
        wg = jnp.concatenate([wmi[:, o_q:o_alr], wmi[:, o_alr:o_rin],
                              jnp.zeros((D, 128 - 2 * GLA_RANK), F32)], axis=1).astype(BF16)
        wr = wmi[:, o_rin:o_c].astype(BF16)
        wc = wmi[:, o_c:].astype(BF16)

        x = _ffn(x, mod[l], 0, g_pre[l, 0], g_post[l, 0],
                 ffn1_w_in[l].astype(BF16), ffn1_w_out[l].astype(BF16), 0.5)

        pf, pg, pr, pc = _mix_in(x, mod[l], g_pre[l, 1], wf, wg, wr, wc)
        y_f = _fourier(pf, dft_tables)

        o_rev = _gla_pass(pg, None, gla_w_alpha[l], gla_b_alpha[l], gla_norm_g[l], rev=True, final=False)
        y_g = _gla_pass(pg, o_rev, gla_w_alpha[l], gla_b_alpha[l], gla_norm_g[l], rev=False, final=True)

        wa_bd = _block_diag(lru_w_a[l]).astype(BF16)
        wx_bd = _block_diag(lru_w_x[l]).astype(BF16)
        lru_args = (lru_conv_w[l], lru_conv_b[l], wa_bd, lru_b_a[l], wx_bd, lru_b_x[l], lru_lambda[l])
        h_rev = _lru_pass(pr, None, *lru_args, rev=True, final=False)
        y_r = _lru_pass(pr, h_rev, *lru_args, rev=False, final=True)

        y_c = _conformer(pc, conf_dw_w[l], conf_dw_b[l], conf_ln_g[l], conf_ln_b[l])

        x = _mix_out(x, mod[l], g_post[l, 1], y_f, y_g, y_r, y_c, w_mix_out[l].astype(BF16))

        x = _ffn(x, mod[l], 2, g_pre[l, 2], g_post[l, 2],
                 ffn2_w_in[l].astype(BF16), ffn2_w_out[l].astype(BF16), 0.5,
                 split_out=B0 if l == L - 1 else None)

    return tuple(x)
```

```python
import functools
import math

import numpy as np
import jax
import jax.numpy as jnp
from jax import lax
from jax.experimental import pallas as pl
from jax.experimental.pallas import tpu as pltpu

F32 = jnp.float32
BF16 = jnp.bfloat16

EPS = 1e-6
N_SUB = 3
GROUP_WIDTH = 512
FNET_GROUPS = 4
FNET_GROUP_DIM = 128
GLA_HEADS = 4
GLA_DK = 64
GLA_DV = 128
GLA_RANK = 16
GLA_TAU = 16.0
GLA_CHUNK = 64
LRU_BLOCKS = 8
LRU_BLOCK_DIM = 64
LRU_C = 8.0
LRU_TAPS = 4
CONF_TAPS = 31
CONF_HALO = 16
LRU_HALO = 8

V7X_VMEM_BYTES = 64 * 1024 * 1024
VMEM_LIMIT = 52 * 1024 * 1024

NEG_BIG = -1e30


def _cparams(sem):
    return pltpu.CompilerParams(dimension_semantics=sem, vmem_limit_bytes=VMEM_LIMIT)


def _sigmoid(x):
    return 1.0 / (1.0 + jnp.exp(-x))


def _silu(x):
    return x * _sigmoid(x)


def _rms(x, g):
    return x * lax.rsqrt(jnp.mean(x * x, axis=-1, keepdims=True) + EPS) * g


ROW_CHUNK = 16


LANES = 128


def _for_row_chunks(n_rows, body, unroll):
    def step(c, carry):
        body(pl.ds(pl.multiple_of(c * ROW_CHUNK, ROW_CHUNK), ROW_CHUNK))
        return carry
    lax.fori_loop(0, n_rows // ROW_CHUNK, step, 0, unroll=unroll)


def _inv_rms_rows(src_ref, rs_ref):
    def body(rows):
        x = src_ref[rows, :]
        ms = jnp.mean(x * x, axis=-1, keepdims=True)
        rs_ref[rows, :] = jnp.broadcast_to(lax.rsqrt(ms + EPS), (ROW_CHUNK, LANES))
    _for_row_chunks(src_ref.shape[0], body, unroll=8)


def _store_row_vectors(vec_ref, *vectors):
    for j, v in enumerate(vectors):
        vec_ref[j] = jnp.broadcast_to(v, vec_ref.shape[1:])


def _scaled_rows(src_ref, rs_ref, rows):
    x = src_ref[rows, :]
    return x * jnp.tile(rs_ref[rows, :], (1, x.shape[-1] // LANES))


def _dot(a, b):
    return jnp.dot(a, b, preferred_element_type=F32)


def _dot_nt(a, b):
    return lax.dot_general(a, b, (((1,), (1,)), ((), ())), preferred_element_type=F32)


def _dot_tn(a, b):
    return lax.dot_general(a, b, (((0,), (0,)), ((), ())), preferred_element_type=F32)


def _mod_kernel(c_ref, w_ref, b_ref, o_ref):
    c = c_ref[...]
    o_ref[...] = jnp.dot(_silu(c), w_ref[...], preferred_element_type=F32,
                         precision=lax.Precision.HIGHEST) + b_ref[...]


def _adaln_mod(c_pad, w_ada, b_ada):
    L, D, W = w_ada.shape
    NP = c_pad.shape[0]
    tn = 1024
    return pl.pallas_call(
        _mod_kernel,
        grid=(L, W // tn),
        in_specs=[
            pl.BlockSpec((NP, D), lambda l, j: (0, 0)),
            pl.BlockSpec((None, D, tn), lambda l, j: (l, 0, j)),
            pl.BlockSpec((None, 1, tn), lambda l, j: (l, 0, j)),
        ],
        out_specs=pl.BlockSpec((None, NP, tn), lambda l, j: (l, 0, j)),
        out_shape=jax.ShapeDtypeStruct((L, NP, W), F32),
        compiler_params=_cparams(("arbitrary", "arbitrary")),
        name="adaln_mod",
    )(c_pad, w_ada, b_ada.reshape(L, 1, W))


def _ffn_kernel(*refs, res_w, n_in, n_out, n_split):
    x_refs = refs[:n_in]
    sh_ref, sc_ref, gt_ref, gpre_ref, gpost_ref, wu_ref, wg_ref, wo_ref = refs[n_in:n_in + 8]
    o_refs = refs[n_in + 8:n_in + 8 + n_out]
    xn_ref, acc_ref, rs_ref, vec_ref = refs[n_in + 8 + n_out:]
    n = pl.program_id(0)
    f = pl.program_id(2)
    is_first, is_last = f == 0, f == pl.num_programs(2) - 1
    in_second = n >= n_split

    def prologue(x_ref):
        _inv_rms_rows(x_ref, rs_ref)
        _store_row_vectors(vec_ref, gpre_ref[...] * (1.0 + sc_ref[...]), sh_ref[...])

        def body(rows):
            xn = _scaled_rows(x_ref, rs_ref, rows) * vec_ref[0] + vec_ref[1]
            xn_ref[rows, :] = xn.astype(BF16)
        _for_row_chunks(x_ref.shape[0], body, unroll=4)
        acc_ref[...] = jnp.zeros_like(acc_ref)

    if n_in == 1:
        pl.when(is_first)(lambda: prologue(x_refs[0]))
    else:
        pl.when(is_first & jnp.logical_not(in_second))(lambda: prologue(x_refs[0]))
        pl.when(is_first & in_second)(lambda: prologue(x_refs[1]))

    xn = xn_ref[...]
    up = _dot(xn, wu_ref[...])
    gate = _dot(xn, wg_ref[...])
    act = (_silu(gate) * up).astype(BF16)
    acc_ref[...] += _dot(act, wo_ref[...])

    def epilogue(x_ref, o_ref):
        _inv_rms_rows(acc_ref, rs_ref)
        _store_row_vectors(vec_ref, (res_w * gt_ref[...]) * gpost_ref[...])

        def body(rows):
            o_ref[rows, :] = x_ref[rows, :] + _scaled_rows(acc_ref, rs_ref, rows) * vec_ref[0]
        _for_row_chunks(x_ref.shape[0], body, unroll=4)

    if n_in == 1 and n_out == 1:
        pl.when(is_last)(lambda: epilogue(x_refs[0], o_refs[0]))
    else:
        pl.when(is_last & jnp.logical_not(in_second))(lambda: epilogue(x_refs[0], o_refs[0]))
        pl.when(is_last & in_second)(lambda: epilogue(x_refs[-1], o_refs[-1]))


def _ffn(xs, mod_l, sub, g_pre, g_post, w_in, w_out, res_w, split_out=None):
    xs = xs if isinstance(xs, (tuple, list)) else (xs,)
    S, D = xs[0].shape[1:]
    N = sum(a.shape[0] for a in xs)
    n_split = xs[0].shape[0] if len(xs) == 2 else (split_out if split_out else N)
    FF = w_out.shape[0]
    tm = min(512, S)
    tf = 512
    nf = FF // tf
    ni = S // tm

    def first_map(n, i, f):
        return (jnp.minimum(n, n_split - 1), jnp.where(n < n_split, i, ni - 1), 0)

    def second_map(n, i, f):
        return (jnp.maximum(n - n_split, 0), jnp.where(n >= n_split, i, 0), 0)

    whole_map = lambda n, i, f: (n, i, 0)
    tok = lambda imap: pl.BlockSpec((None, tm, D), imap)
    x_specs = [tok(whole_map)] if len(xs) == 1 else [tok(first_map), tok(second_map)]
    if split_out:
        o_specs = [tok(first_map), tok(second_map)]
        o_shapes = [jax.ShapeDtypeStruct((n_split, S, D), F32), jax.ShapeDtypeStruct((N - n_split, S, D), F32)]
    else:
        o_specs = [tok(whole_map)]
        o_shapes = [jax.ShapeDtypeStruct((N, S, D), F32)]
    mspec = lambda r: pl.BlockSpec((None, None, 1, D), lambda n, i, f: (n, 3 * sub + r, 0, 0))
    out = pl.pallas_call(
        functools.partial(_ffn_kernel, res_w=res_w, n_in=len(xs), n_out=len(o_specs), n_split=n_split),
        grid=(N, ni, nf),
        in_specs=x_specs + [
            mspec(0), mspec(1), mspec(2),
            pl.BlockSpec((1, D), lambda n, i, f: (0, 0)),
            pl.BlockSpec((1, D), lambda n, i, f: (0, 0)),
            pl.BlockSpec((D, tf), lambda n, i, f: (0, f)),
            pl.BlockSpec((D, tf), lambda n, i, f: (0, f + nf)),
            pl.BlockSpec((tf, D), lambda n, i, f: (f, 0)),
        ],
        out_specs=o_specs,
        out_shape=o_shapes,
        scratch_shapes=[pltpu.VMEM((tm, D), BF16), pltpu.VMEM((tm, D), F32), pltpu.VMEM((tm, LANES), F32),
                        pltpu.VMEM((2, ROW_CHUNK, D), F32)],
        compiler_params=_cparams(("arbitrary", "arbitrary", "arbitrary")),
        name="ffn",
    )(*xs, mod_l, mod_l, mod_l, g_pre.reshape(1, D), g_post.reshape(1, D), w_in, w_in, w_out)
    return tuple(out) if split_out else out[0]


def _mix_in_kernel(x_ref, sh_ref, sc_ref, gpre_ref, wf_ref, wg_ref, wr_ref, wc_ref,
                   pf_ref, pg_ref, pr_ref, pc_ref, xn_ref, rs_ref, vec_ref):
    _inv_rms_rows(x_ref, rs_ref)
    _store_row_vectors(vec_ref, gpre_ref[...] * (1.0 + sc_ref[...]), sh_ref[...])

    def body(rows):
        xn = _scaled_rows(x_ref, rs_ref, rows) * vec_ref[0] + vec_ref[1]
        xn_ref[rows, :] = xn.astype(BF16)
    _for_row_chunks(x_ref.shape[0], body, unroll=4)
    xn = xn_ref[...]
    pf_ref[...] = _dot(xn, wf_ref[...])
    pg_ref[...] = _dot(xn, wg_ref[...])
    pr_ref[...] = _dot(xn, wr_ref[...])
    pc_ref[...] = _dot(xn, wc_ref[...])


def _mix_in(x, mod_l, g_pre, wf, wg, wr, wc):
    N, S, D = x.shape
    tm = min(256, S)
    sub = 1
    mspec = lambda r: pl.BlockSpec((None, None, 1, D), lambda n, i: (n, 3 * sub + r, 0, 0))
    const = lambda shape: pl.BlockSpec(shape, lambda n, i: (0, 0), pipeline_mode=pl.Buffered(1))
    tok = lambda w: pl.BlockSpec((None, tm, w), lambda n, i: (n, i, 0))
    widths = [w.shape[1] for w in (wf, wg, wr, wc)]
    return pl.pallas_call(
        _mix_in_kernel,
        grid=(N, S // tm),
        in_specs=[tok(D), mspec(0), mspec(1), const((1, D)),
                  const(wf.shape), const(wg.shape), const(wr.shape), const(wc.shape)],
        out_specs=[tok(w) for w in widths],
        out_shape=[jax.ShapeDtypeStruct((N, S, w), F32) for w in widths],
        scratch_shapes=[pltpu.VMEM((tm, D), BF16), pltpu.VMEM((tm, LANES), F32),
                        pltpu.VMEM((2, ROW_CHUNK, D), F32)],
        compiler_params=_cparams(("arbitrary", "arbitrary")),
        name="mix_in",
    )(x, mod_l, mod_l, g_pre.reshape(1, D), wf, wg, wr, wc)


DFT_B = 128
DFT_PAD = 8
DFT_UNROLL = 4


def _fourier_kernel(x_ref, t1_ref, m2_ref, cs_ref, o_ref, vs_ref, ys_ref, *, A):
    B = DFT_B
    pv = A + DFT_PAD
    py = B + DFT_PAD
    GW = FNET_GROUP_DIM

    def stage1(t1, carry):
        xs = x_ref[pl.ds(t1, B, stride=A), :].astype(BF16)
        v = _dot(t1_ref[t1], xs)
        vs_ref[0, pl.ds(t1, B, stride=pv), :] = v[:B]
        vs_ref[1, pl.ds(t1, B, stride=pv), :] = v[B:]
        return carry

    lax.fori_loop(0, A, stage1, 0, unroll=DFT_UNROLL)
    m2 = m2_ref[...]

    def stage2(f2, carry):
        r0 = pl.multiple_of(f2 * pv, 8)
        r = jnp.concatenate([vs_ref[0, pl.ds(r0, A), :], vs_ref[1, pl.ds(r0, A), :]], axis=0)
        y = _dot(m2, r.astype(BF16))
        ys_ref[0, pl.ds(f2, A, stride=py), :] = y[:A]
        ys_ref[1, pl.ds(f2, A, stride=py), :] = y[A:]
        return carry

    lax.fori_loop(0, B, stage2, 0, unroll=DFT_UNROLL)
    c128 = cs_ref[pl.ds(0, GW), :]
    s128 = cs_ref[pl.ds(GW, GW), :]

    def stage3(f1, carry):
        r0 = pl.multiple_of(f1 * py, 8)
        out = _dot(ys_ref[0, pl.ds(r0, B), :].astype(BF16), c128)
        out += _dot(ys_ref[1, pl.ds(r0, B), :].astype(BF16), s128)
        o_ref[pl.ds(pl.multiple_of(f1 * B, B), B), :] = out.astype(o_ref.dtype)
        return carry

    lax.fori_loop(0, A, stage3, 0, unroll=DFT_UNROLL)


def _fourier_tables(S):
    B = DFT_B
    A = S // B
    t1 = np.arange(A)[:, None, None]
    f2 = np.arange(B)[None, :, None]
    t2 = np.arange(B)[None, None, :]
    ang = 2.0 * np.pi * ((f2 * (t1 + A * t2)) % S) / S
    tab1 = np.concatenate([np.cos(ang), -np.sin(ang)], axis=1)
    a = np.arange(A)
    ang_a = 2.0 * np.pi * ((a[:, None] * a[None, :]) % A) / A
    ca, sa = np.cos(ang_a), np.sin(ang_a)
    m2 = np.block([[ca, sa], [-sa, ca]])
    c = np.arange(FNET_GROUP_DIM)
    ang_c = 2.0 * np.pi * ((c[:, None] * c[None, :]) % FNET_GROUP_DIM) / FNET_GROUP_DIM
    scale = (S * FNET_GROUP_DIM) ** -0.5
    cs = np.concatenate([np.cos(ang_c), np.sin(ang_c)], axis=0) * scale
    return jnp.asarray(tab1, BF16), jnp.asarray(m2, BF16), jnp.asarray(cs, BF16)


def _fourier(pf, tables):
    N, S, W = pf.shape
    GW = FNET_GROUP_DIM
    B = DFT_B
    A = S // B
    tab1, m2, cs = tables
    const = lambda a: pl.BlockSpec(a.shape, lambda n, g: (0,) * a.ndim, pipeline_mode=pl.Buffered(1))
    return pl.pallas_call(
        functools.partial(_fourier_kernel, A=A),
        grid=(N, W // GW),
        in_specs=[pl.BlockSpec((None, S, GW), lambda n, g: (n, 0, g)), const(tab1), const(m2), const(cs)],
        out_specs=pl.BlockSpec((None, S, GW), lambda n, g: (n, 0, g)),
        out_shape=jax.ShapeDtypeStruct((N, S, W), BF16),
        scratch_shapes=[pltpu.VMEM((2, B * (A + DFT_PAD), GW), F32),
                        pltpu.VMEM((2, A * (B + DFT_PAD), GW), F32)],
        compiler_params=_cparams(("arbitrary", "arbitrary")),
        name="fourier",
    )(pf, tab1, m2, cs)


def _gla_tables(rev):
    C = GLA_CHUNK
    idx = np.arange(C)
    if not rev:
        cum = (idx[None, :] <= idx[:, None]).astype(np.float32)
    else:
        cum = (idx[None, :] >= idx[:, None]).astype(np.float32)
    mats = []
    m = C // 2
    while m >= 1:
        base = (idx // (2 * m)) * (2 * m)
        piv = base + (m if rev else m - 1)
        mats.append(cum - cum[piv, :])
        m //= 2
    mats.append(cum)
    return np.concatenate(mats, axis=0)


def _split3(x):
    hi = x.astype(BF16)
    r1 = x - hi.astype(F32)
    mid = r1.astype(BF16)
    lo = (r1 - mid.astype(F32)).astype(BF16)
    return hi, mid, lo


def _gla_kernel(*refs, rev, final, nchunks):
    q_ref, k_ref, v_ref, alr_ref = refs[:4]
    og_ref, oth_ref = refs[4:6] if final else (None, None)
    m_ref, wa_ref, ba_ref, g_ref, o_ref, st_ref = refs[-6:]
    C, H, DK, DV = GLA_CHUNK, GLA_HEADS, GLA_DK, GLA_DV
    HK = H * DK

    @pl.when(pl.program_id(1) == 0)
    def _():
        st_ref[...] = jnp.zeros_like(st_ref)

    row = lax.broadcasted_iota(jnp.int32, (C, HK), 0)
    head_of_lane = lax.broadcasted_iota(jnp.int32, (C, HK), 1) >> int(math.log2(DK))
    t_i = lax.broadcasted_iota(jnp.int32, (H * C, C), 0) & (C - 1)
    s_i = lax.broadcasted_iota(jnp.int32, (H * C, C), 1)
    head_of_state_lane = lax.broadcasted_iota(jnp.int32, (DV, HK), 1) >> int(math.log2(DK))
    mstack = m_ref[...]
    d0 = (GLA_RANK if rev else 0)

    def stack_heads(x):
        return jnp.concatenate([jnp.where(head_of_lane == h, x, 0.0) for h in range(H)],
                               axis=0).astype(BF16)

    chunks = [slice(c * C, (c + 1) * C) for c in range(nchunks)]

    a16 = alr_ref[...][:, d0:d0 + GLA_RANK]
    z = jnp.dot(a16, wa_ref[...], preferred_element_type=F32,
                precision=lax.Precision.HIGHEST) + ba_ref[...]
    la = (jnp.minimum(z, 0.0) - jnp.log1p(jnp.exp(-jnp.abs(z)))) / GLA_TAU
    hi, mid, lo = _split3(la)
    e_all = [_dot(mstack, hi[r]) + _dot(mstack, mid[r]) + _dot(mstack, lo[r]) for r in chunks]
    q_all = q_ref[...] * (DK ** -0.5)
    k_all = k_ref[...]
    v_all = v_ref[...].astype(BF16)

    scores = []
    for c, r in enumerate(chunks):
        q, k = q_all[r], k_all[r]
        sc = jnp.zeros((H * C, C), F32)
        m = C // 2
        lvl = 0
        while m >= 1:
            d = e_all[c][lvl * C:(lvl + 1) * C]
            upper = (row & (2 * m - 1)) >= m
            q_act, k_act = (jnp.logical_not(upper), upper) if rev else (upper, jnp.logical_not(upper))
            qh = q * jnp.exp(jnp.where(q_act, d, NEG_BIG))
            kh = (k * jnp.exp(jnp.where(k_act, -d, NEG_BIG))).astype(BF16)
            sh = int(math.log2(2 * m))
            same = (t_i >> sh) == (s_i >> sh)
            sc = sc + jnp.where(same, _dot_nt(stack_heads(qh), kh), 0.0)
            m //= 2
            lvl += 1
        sc = sc + jnp.where(t_i == s_i, _dot_nt(stack_heads(q), k.astype(BF16)), 0.0)
        scores.append(sc.astype(BF16))

    o_intra = [[_dot(scores[c][h * C:(h + 1) * C], v_all[r, h * DV:(h + 1) * DV]) for h in range(H)]
               for c, r in enumerate(chunks)]

    st = st_ref[...]
    for c in (reversed(range(nchunks)) if rev else range(nchunks)):
        r = chunks[c]
        q, k, v = q_all[r], k_all[r], v_all[r]
        b = e_all[c][6 * C:7 * C]
        b_edge = b[0:1, :] if rev else b[C - 1:C, :]
        kd = (k * jnp.exp(b_edge - b)).astype(BF16)
        o_inter = _dot_nt(stack_heads(q * jnp.exp(b)), st.astype(BF16))
        upd = _dot_tn(v, kd)
        st = st * jnp.exp(b_edge)
        for h in range(H):
            st = st + jnp.where(head_of_state_lane == h, upd[h * DV:(h + 1) * DV], 0.0)
        outs = []
        for h in range(H):
            o = o_intra[c][h] + o_inter[h * C:(h + 1) * C]
            if final:
                tot = o + oth_ref[r, h * DV:(h + 1) * DV]
                y = _rms(tot, g_ref[...]) * _silu(og_ref[r, h * DV:(h + 1) * DV])
                outs.append(y.astype(o_ref.dtype))
            else:
                outs.append(o)
        o_ref[r, :] = jnp.concatenate(outs, axis=-1)
    st_ref[...] = st


def _gla_pass(pg, other, w_alpha, b_alpha, norm_g, *, rev, final):
    N, S, _ = pg.shape
    HK, HV = GLA_HEADS * GLA_DK, GLA_HEADS * GLA_DV
    tb = min(256, S)
    nb = S // tb
    blk = (lambda i: nb - 1 - i) if rev else (lambda i: i)
    tok = lambda w, j: pl.BlockSpec((None, tb, w), lambda n, i: (n, blk(i), j))
    const = lambda shape: pl.BlockSpec(shape, lambda n, i: (0,) * len(shape))
    mstack = jnp.asarray(_gla_tables(rev), BF16)
    d = 1 if rev else 0
    out_dtype = BF16 if final else F32
    extra_specs = [tok(HV, 2), tok(HV, 0)] if final else []
    extra_args = (pg, other) if final else ()
    return pl.pallas_call(
        functools.partial(_gla_kernel, rev=rev, final=final, nchunks=tb // GLA_CHUNK),
        grid=(N, nb),
        in_specs=[tok(HK, 0), tok(HK, 1), tok(HV, 1), tok(128, (2 * HK + 2 * HV) // 128)]
                 + extra_specs
                 + [const(mstack.shape), const((GLA_RANK, HK)), const((1, HK)), const((1, GLA_DV))],
        out_specs=pl.BlockSpec((None, tb, HV), lambda n, i: (n, blk(i), 0)),
        out_shape=jax.ShapeDtypeStruct((N, S, HV), out_dtype),
        scratch_shapes=[pltpu.VMEM((GLA_DV, HK), F32)],
        compiler_params=_cparams(("arbitrary", "arbitrary")),
        name="gla_rev" if rev else "gla_fwd",
    )(pg, pg, pg, pg, *extra_args, mstack, w_alpha[d], b_alpha[d].reshape(1, HK),
      norm_g.reshape(1, GLA_DV))


def _softplus(x):
    return jnp.maximum(x, 0.0) + jnp.log1p(jnp.exp(-jnp.abs(x)))


def _gelu_tanh(x):
    return 0.5 * x * (1.0 + jnp.tanh(math.sqrt(2.0 / math.pi) * (x + 0.044715 * (x * x * x))))


def _lru_kernel(*refs, rev, final, tb):
    x_ref, halo_ref = refs[:2]
    rg_ref, oth_ref = refs[2:4] if final else (None, None)
    cw_ref, cb_ref, wa_ref, ba_ref, wx_ref, bx_ref, lam_ref, o_ref, ext_ref, car_ref = refs[-10:]
    W = x_ref.shape[-1]
    i = pl.program_id(1)
    first = i == 0

    @pl.when(first)
    def _():
        car_ref[...] = jnp.zeros_like(car_ref)

    halo = jnp.where(first, 0.0, halo_ref[...])
    if rev:
        ext_ref[pl.ds(0, tb), :] = x_ref[...]
        ext_ref[pl.ds(tb, LRU_HALO), :] = halo
    else:
        ext_ref[pl.ds(0, LRU_HALO), :] = halo
        ext_ref[pl.ds(LRU_HALO, tb), :] = x_ref[...]
    cw = cw_ref[...]
    xc = jnp.zeros((tb, W), F32) + cb_ref[...]
    for tap in range(LRU_TAPS):
        off = (LRU_TAPS - 1 - tap) if rev else (LRU_HALO - (LRU_TAPS - 1) + tap)
        xc = xc + cw[tap:tap + 1, :] * ext_ref[pl.ds(off, tb), :]

    xcb = xc.astype(BF16)
    r = _sigmoid(_dot(xcb, wa_ref[...]) + ba_ref[...])
    g = _sigmoid(_dot(xcb, wx_ref[...]) + bx_ref[...])
    log_a = (-LRU_C) * r * _softplus(-lam_ref[...])
    a = jnp.exp(log_a)
    h = jnp.sqrt(1.0 - jnp.exp(2.0 * log_a)) * (g * xc)

    row = lax.broadcasted_iota(jnp.int32, (tb, W), 0)
    d = 1
    while d < tb:
        if rev:
            valid = row < tb - d
            shift = tb - d
        else:
            valid = row >= d
            shift = d
        a_s = jnp.where(valid, pltpu.roll(a, shift, 0), 1.0)
        h_s = jnp.where(valid, pltpu.roll(h, shift, 0), 0.0)
        h = h + a * h_s
        a = a * a_s
        d *= 2
    h = h + a * car_ref[...]
    car_ref[...] = h[0:1, :] if rev else h[tb - 1:tb, :]
    if final:
        o_ref[...] = ((h + oth_ref[...]) * _gelu_tanh(rg_ref[...])).astype(o_ref.dtype)
    else:
        o_ref[...] = h


def _lru_pass(pr, other, conv_w, conv_b, wa_bd, b_a, wx_bd, b_x, lam, *, rev, final):
    N, S, _ = pr.shape
    W = GROUP_WIDTH
    tb = min(256, S)
    nb = S // tb
    hb = tb // LRU_HALO
    blk = (lambda i: nb - 1 - i) if rev else (lambda i: i)
    if rev:
        halo_blk = lambda i: jnp.minimum((blk(i) + 1) * hb, S // LRU_HALO - 1)
    else:
        halo_blk = lambda i: jnp.maximum(blk(i) * hb - 1, 0)
    tok = lambda j: pl.BlockSpec((None, tb, W), lambda n, i: (n, blk(i), j))
    const = lambda shape: pl.BlockSpec(shape, lambda n, i: (0,) * len(shape))
    out_dtype = BF16 if final else F32
    d = 1 if rev else 0
    extra_specs = [tok(1), tok(0)] if final else []
    extra_args = (pr, other) if final else ()
    return pl.pallas_call(
        functools.partial(_lru_kernel, rev=rev, final=final, tb=tb),
        grid=(N, nb),
        in_specs=[tok(0),
                  pl.BlockSpec((None, LRU_HALO, W), lambda n, i: (n, halo_blk(i), 0))]
                 + extra_specs
                 + [const((LRU_TAPS, W)), const((1, W)), const((W, W)), const((1, W)),
                    const((W, W)), const((1, W)), const((1, W))],
        out_specs=pl.BlockSpec((None, tb, W), lambda n, i: (n, blk(i), 0)),
        out_shape=jax.ShapeDtypeStruct((N, S, W), out_dtype),
        scratch_shapes=[pltpu.VMEM((tb + LRU_HALO, W), F32), pltpu.VMEM((1, W), F32)],
        compiler_params=_cparams(("arbitrary", "arbitrary")),
        name="lru_rev" if rev else "lru_fwd",
    )(pr, pr, *extra_args, conv_w[d], conv_b[d].reshape(1, W), wa_bd[d], b_a[d].reshape(1, W),
      wx_bd[d], b_x[d].reshape(1, W), lam[d].reshape(1, W))


def _conf_kernel(x_ref, prev_ref, next_ref, cw_ref, cb_ref, lg_ref, lb_ref, o_ref, ext_ref, sh_ref, *, tb):
    W = GROUP_WIDTH
    i = pl.program_id(1)
    last = pl.num_programs(1) - 1

    def glu(t):
        return t[:, :W] * _sigmoid(t[:, W:])

    ext_ref[pl.ds(0, CONF_HALO), :] = jnp.where(i == 0, 0.0, glu(prev_ref[...]))
    ext_ref[pl.ds(CONF_HALO, tb), :] = glu(x_ref[...])
    ext_ref[pl.ds(CONF_HALO + tb, CONF_HALO), :] = jnp.where(i == last, 0.0, glu(next_ref[...]))
    span = tb + 2 * CONF_HALO - 8
    for r in range(1, 8):
        sh_ref[r - 1] = ext_ref[pl.ds(r, span), :]
    cw = cw_ref[...]
    u = jnp.zeros((tb, W), F32) + cb_ref[...]
    base = CONF_HALO - CONF_TAPS // 2
    for tap in range(CONF_TAPS):
        q, r = divmod(base + tap, 8)
        src = ext_ref[pl.ds(8 * q, tb), :] if r == 0 else sh_ref[r - 1, pl.ds(8 * q, tb), :]
        u = u + cw[tap:tap + 1, :] * src
    uc = u - jnp.mean(u, axis=-1, keepdims=True)
    y = uc * lax.rsqrt(jnp.mean(uc * uc, axis=-1, keepdims=True) + EPS) * lg_ref[...] + lb_ref[...]
    o_ref[...] = _silu(y).astype(o_ref.dtype)


def _conformer(pc, dw_w, dw_b, ln_g, ln_b):
    N, S, W2 = pc.shape
    W = GROUP_WIDTH
    tb = min(256, S)
    nb = S // tb
    hb = tb // CONF_HALO
    taps_pad = 32
    cw = jnp.zeros((taps_pad, W), F32).at[:CONF_TAPS].set(dw_w)
    const = lambda shape: pl.BlockSpec(shape, lambda n, i: (0,) * len(shape))
    return pl.pallas_call(
        functools.partial(_conf_kernel, tb=tb),
        grid=(N, nb),
        in_specs=[pl.BlockSpec((None, tb, W2), lambda n, i: (n, i, 0)),
                  pl.BlockSpec((None, CONF_HALO, W2), lambda n, i: (n, jnp.maximum(i * hb - 1, 0), 0)),
                  pl.BlockSpec((None, CONF_HALO, W2),
                               lambda n, i: (n, jnp.minimum((i + 1) * hb, S // CONF_HALO - 1), 0)),
                  const((taps_pad, W)), const((1, W)), const((1, W)), const((1, W))],
        out_specs=pl.BlockSpec((None, tb, W), lambda n, i: (n, i, 0)),
        out_shape=jax.ShapeDtypeStruct((N, S, W), BF16),
        scratch_shapes=[pltpu.VMEM((tb + 2 * CONF_HALO, W), F32),
                        pltpu.VMEM((7, tb + 2 * CONF_HALO - 8, W), F32)],
        compiler_params=_cparams(("arbitrary", "arbitrary")),
        name="conformer",
    )(pc, pc, pc, cw, dw_b.reshape(1, W), ln_g.reshape(1, W), ln_b.reshape(1, W))


def _mix_out_kernel(x_ref, gt_ref, gpost_ref, yf_ref, yg_ref, yr_ref, yc_ref, w_ref, o_ref, acc_ref, rs_ref,
                    vec_ref):
    W = GROUP_WIDTH
    acc = _dot(yf_ref[...], w_ref[pl.ds(0, W), :])
    acc += _dot(yg_ref[...], w_ref[pl.ds(W, W), :])
    acc += _dot(yr_ref[...], w_ref[pl.ds(2 * W, W), :])
    acc += _dot(yc_ref[...], w_ref[pl.ds(3 * W, W), :])
    acc_ref[...] = acc
    _inv_rms_rows(acc_ref, rs_ref)
    _store_row_vectors(vec_ref, gt_ref[...] * gpost_ref[...])

    def body(rows):
        o_ref[rows, :] = x_ref[rows, :] + _scaled_rows(acc_ref, rs_ref, rows) * vec_ref[0]
    _for_row_chunks(x_ref.shape[0], body, unroll=4)


def _mix_out(x, mod_l, g_post, y_f, y_g, y_r, y_c, w_out):
    N, S, D = x.shape
    W = GROUP_WIDTH
    tm = min(512, S)
    sub = 1
    tok = lambda w: pl.BlockSpec((None, tm, w), lambda n, i: (n, i, 0))
    return pl.pallas_call(
        _mix_out_kernel,
        grid=(N, S // tm),
        in_specs=[tok(D),
                  pl.BlockSpec((None, None, 1, D), lambda n, i: (n, 3 * sub + 2, 0, 0)),
                  pl.BlockSpec((1, D), lambda n, i: (0, 0)),
                  tok(W), tok(W), tok(W), tok(W),
                  pl.BlockSpec(w_out.shape, lambda n, i: (0, 0), pipeline_mode=pl.Buffered(1))],
        out_specs=tok(D),
        out_shape=jax.ShapeDtypeStruct((N, S, D), F32),
        scratch_shapes=[pltpu.VMEM((tm, D), F32), pltpu.VMEM((tm, LANES), F32),
                        pltpu.VMEM((1, ROW_CHUNK, D), F32)],
        compiler_params=_cparams(("arbitrary", "arbitrary")),
        name="mix_out",
    )(x, mod_l, g_post.reshape(1, D), y_f, y_g, y_r, y_c, w_out)


def _block_diag(w):
    nd, nblk, bd, _ = w.shape
    out = jnp.zeros((nd, nblk * bd, nblk * bd), w.dtype)
    for b in range(nblk):
        out = out.at[:, b * bd:(b + 1) * bd, b * bd:(b + 1) * bd].set(w[:, b])
    return out


def kernel(x_prompt, x_sample, c_prompt, c_sample, w_ada, b_ada, g_pre, g_post, ffn1_w_in, ffn1_w_out, ffn2_w_in, ffn2_w_out, w_mix_in, gla_w_alpha, gla_b_alpha, gla_norm_g, lru_conv_w, lru_conv_b, lru_w_a, lru_b_a, lru_w_x, lru_b_x, lru_lambda, conf_dw_w, conf_dw_b, conf_ln_g, conf_ln_b, w_mix_out):
    assert x_prompt.shape[1:] == x_sample.shape[1:]
    B0 = x_prompt.shape[0]
    N = B0 + x_sample.shape[0]
    S, D = x_prompt.shape[1:]
    L = w_ada.shape[0]
    NP = -(-N // 8) * 8
    c = jnp.zeros((NP, D), F32).at[:N].set(jnp.concatenate([c_prompt, c_sample], axis=0))

    mod = _adaln_mod(c, w_ada, b_ada).reshape(L, NP, 3 * N_SUB, 1, D)
    dft_tables = _fourier_tables(S)
    x = (x_prompt, x_sample)

    W = GROUP_WIDTH
    HK, HV = GLA_HEADS * GLA_DK, GLA_HEADS * GLA_DV
    o_q = W
    o_k = o_q + HK
    o_v = o_k + HK
    o_og = o_v + HV
    o_alr = o_og + HV
    o_rin = o_alr + 2 * GLA_RANK
    o_rg = o_rin + W
    o_c = o_rg + W

    for l in range(L):
        wmi = w_mix_in[l]
        wf = wmi[:, :W].astype(BF16)
        wg = jnp.concatenate([wmi[:, o_q:o_alr], wmi[:, o_alr:o_rin],
                              jnp.zeros((D, 128 - 2 * GLA_RANK), F32)], axis=1).astype(BF16)
        wr = wmi[:, o_rin:o_c].astype(BF16)
        wc = wmi[:, o_c:].astype(BF16)

        x = _ffn(x, mod[l], 0, g_pre[l, 0], g_post[l, 0],
                 ffn1_w_in[l].astype(BF16), ffn1_w_out[l].astype(BF16), 0.5)

        pf, pg, pr, pc = _mix_in(x, mod[l], g_pre[l, 1], wf, wg, wr, wc)
        y_f = _fourier(pf, dft_tables)

        o_rev = _gla_pass(pg, None, gla_w_alpha[l], gla_b_alpha[l], gla_norm_g[l], rev=True, final=False)
        y_g = _gla_pass(pg, o_rev, gla_w_alpha[l], gla_b_alpha[l], gla_norm_g[l], rev=False, final=True)

        wa_bd = _block_diag(lru_w_a[l]).astype(BF16)
        wx_bd = _block_diag(lru_w_x[l]).astype(BF16)
        lru_args = (lru_conv_w[l], lru_conv_b[l], wa_bd, lru_b_a[l], wx_bd, lru_b_x[l], lru_lambda[l])
        h_rev = _lru_pass(pr, None, *lru_args, rev=True, final=False)
        y_r = _lru_pass(pr, h_rev, *lru_args, rev=False, final=True)

        y_c = _conformer(pc, conf_dw_w[l], conf_dw_b[l], conf_ln_g[l], conf_ln_b[l])

        x = _mix_out(x, mod[l], g_post[l, 1], y_f, y_g, y_r, y_c, w_mix_out[l].astype(BF16))

        x = _ffn(x, mod[l], 2, g_pre[l, 2], g_post[l, 2],
                 ffn2_w_in[l].astype(BF16), ffn2_w_out[l].astype(BF16), 0.5,
                 split_out=B0 if l == L - 1 else None)

    return tuple(x)
```

```python
import functools
import math

import numpy as np
import jax
import jax.numpy as jnp
from jax import lax
from jax.experimental import pallas as pl
from jax.experimental.pallas import tpu as pltpu

F32 = jnp.float32
BF16 = jnp.bfloat16

EPS = 1e-6
N_SUB = 3
GROUP_WIDTH = 512
FNET_GROUPS = 4
FNET_GROUP_DIM = 128
GLA_HEADS = 4
GLA_DK = 64
GLA_DV = 128
GLA_RANK = 16
GLA_TAU = 16.0
GLA_CHUNK = 64
LRU_BLOCKS = 8
LRU_BLOCK_DIM = 64
LRU_C = 8.0
LRU_TAPS = 4
CONF_TAPS = 31
CONF_HALO = 16
LRU_HALO = 8

V7X_VMEM_BYTES = 64 * 1024 * 1024
VMEM_LIMIT = 52 * 1024 * 1024

NEG_BIG = -1e30


def _cparams(sem):
    return pltpu.CompilerParams(dimension_semantics=sem, vmem_limit_bytes=VMEM_LIMIT)


def _sigmoid(x):
    return 1.0 / (1.0 + jnp.exp(-x))


def _sigmoid_tanh(x):
    return 0.5 * jnp.tanh(0.5 * x) + 0.5


def _silu(x):
    return x * _sigmoid(x)


def _rms(x, g):
    return x * lax.rsqrt(jnp.mean(x * x, axis=-1, keepdims=True) + EPS) * g


ROW_CHUNK = 16


LANES = 128


def _for_row_chunks(n_rows, body, unroll):
    def step(c, carry):
        body(pl.ds(pl.multiple_of(c * ROW_CHUNK, ROW_CHUNK), ROW_CHUNK))
        return carry
    lax.fori_loop(0, n_rows // ROW_CHUNK, step, 0, unroll=unroll)


def _inv_rms_rows(src_ref, rs_ref):
    def body(rows):
        x = src_ref[rows, :]
        ms = jnp.mean(x * x, axis=-1, keepdims=True)
        rs_ref[rows, :] = jnp.broadcast_to(lax.rsqrt(ms + EPS), (ROW_CHUNK, LANES))
    _for_row_chunks(src_ref.shape[0], body, unroll=8)


def _store_row_vectors(vec_ref, *vectors):
    for j, v in enumerate(vectors):
        vec_ref[j] = jnp.broadcast_to(v, vec_ref.shape[1:])


def _scaled_rows(src_ref, rs_ref, rows):
    x = src_ref[rows, :]
    return x * jnp.tile(rs_ref[rows, :], (1, x.shape[-1] // LANES))


def _dot(a, b):
    return jnp.dot(a, b, preferred_element_type=F32)


def _dot_nt(a, b):
    return lax.dot_general(a, b, (((1,), (1,)), ((), ())), preferred_element_type=F32)


def _dot_tn(a, b):
    return lax.dot_general(a, b, (((0,), (0,)), ((), ())), preferred_element_type=F32)


def _mod_kernel(c_ref, w_ref, b_ref, o_ref):
    c = c_ref[...]
    o_ref[...] = jnp.dot(_silu(c), w_ref[...], preferred_element_type=F32,
                         precision=lax.Precision.HIGHEST) + b_ref[...]


def _adaln_mod(c_pad, w_ada, b_ada):
    L, D, W = w_ada.shape
    NP = c_pad.shape[0]
    tn = 1024
    return pl.pallas_call(
        _mod_kernel,
        grid=(L, W // tn),
        in_specs=[
            pl.BlockSpec((NP, D), lambda l, j: (0, 0)),
            pl.BlockSpec((None, D, tn), lambda l, j: (l, 0, j)),
            pl.BlockSpec((None, 1, tn), lambda l, j: (l, 0, j)),
        ],
        out_specs=pl.BlockSpec((None, NP, tn), lambda l, j: (l, 0, j)),
        out_shape=jax.ShapeDtypeStruct((L, NP, W), F32),
        compiler_params=_cparams(("arbitrary", "arbitrary")),
        name="adaln_mod",
    )(c_pad, w_ada, b_ada.reshape(L, 1, W))


def _ffn_kernel(*refs, res_w, n_in, n_out, n_split):
    x_refs = refs[:n_in]
    sh_ref, sc_ref, gt_ref, gpre_ref, gpost_ref, wu_ref, wg_ref, wo_ref = refs[n_in:n_in + 8]
    o_refs = refs[n_in + 8:n_in + 8 + n_out]
    if n_in == 1 and n_out == 1:
        xn_ref, rs_ref, vec_ref = refs[n_in + 8 + n_out:]
        acc_ref = o_refs[0]
    else:
        xn_ref, acc_ref, rs_ref, vec_ref = refs[n_in + 8 + n_out:]
    n = pl.program_id(0)
    f = pl.program_id(2)
    is_first, is_last = f == 0, f == pl.num_programs(2) - 1
    in_second = n >= n_split

    def prologue(x_ref):
        _inv_rms_rows(x_ref, rs_ref)
        _store_row_vectors(vec_ref, gpre_ref[...] * (1.0 + sc_ref[...]), sh_ref[...])

        def body(rows):
            xn = _scaled_rows(x_ref, rs_ref, rows) * vec_ref[0] + vec_ref[1]
            xn_ref[rows, :] = xn.astype(BF16)
        _for_row_chunks(x_ref.shape[0], body, unroll=4)
        acc_ref[...] = jnp.zeros_like(acc_ref)

    if n_in == 1:
        pl.when(is_first)(lambda: prologue(x_refs[0]))
    else:
        pl.when(is_first & jnp.logical_not(in_second))(lambda: prologue(x_refs[0]))
        pl.when(is_first & in_second)(lambda: prologue(x_refs[1]))

    xn = xn_ref[...]
    up = _dot(xn, wu_ref[...])
    gate = _dot(xn, wg_ref[...])
    act = (_silu(gate) * up).astype(BF16)
    acc_ref[...] += _dot(act, wo_ref[...])

    def epilogue(x_ref, o_ref):
        _inv_rms_rows(acc_ref, rs_ref)
        _store_row_vectors(vec_ref, (res_w * gt_ref[...]) * gpost_ref[...])

        def body(rows):
            o_ref[rows, :] = x_ref[rows, :] + _scaled_rows(acc_ref, rs_ref, rows) * vec_ref[0]
        _for_row_chunks(x_ref.shape[0], body, unroll=4)

    if n_in == 1 and n_out == 1:
        pl.when(is_last)(lambda: epilogue(x_refs[0], o_refs[0]))
    else:
        pl.when(is_last & jnp.logical_not(in_second))(lambda: epilogue(x_refs[0], o_refs[0]))
        pl.when(is_last & in_second)(lambda: epilogue(x_refs[-1], o_refs[-1]))


def _ffn(xs, mod_l, sub, g_pre, g_post, w_in, w_out, res_w, split_out=None):
    xs = xs if isinstance(xs, (tuple, list)) else (xs,)
    S, D = xs[0].shape[1:]
    N = sum(a.shape[0] for a in xs)
    n_split = xs[0].shape[0] if len(xs) == 2 else (split_out if split_out else N)
    FF = w_out.shape[0]
    single = len(xs) == 1 and not split_out
    tm = min(512, S)
    tf = 512
    nf = FF // tf
    ni = S // tm
    acc_scratch = [] if single else [pltpu.VMEM((tm, D), F32)]

    def first_map(n, i, f):
        return (jnp.minimum(n, n_split - 1), jnp.where(n < n_split, i, ni - 1), 0)

    def second_map(n, i, f):
        return (jnp.maximum(n - n_split, 0), jnp.where(n >= n_split, i, 0), 0)

    whole_map = lambda n, i, f: (n, i, 0)
    tok = lambda imap: pl.BlockSpec((None, tm, D), imap)
    x_specs = [tok(whole_map)] if len(xs) == 1 else [tok(first_map), tok(second_map)]
    if split_out:
        o_specs = [tok(first_map), tok(second_map)]
        o_shapes = [jax.ShapeDtypeStruct((n_split, S, D), F32), jax.ShapeDtypeStruct((N - n_split, S, D), F32)]
    else:
        o_specs = [tok(whole_map)]
        o_shapes = [jax.ShapeDtypeStruct((N, S, D), F32)]
    mspec = lambda r: pl.BlockSpec((None, None, 1, D), lambda n, i, f: (n, 3 * sub + r, 0, 0))
    out = pl.pallas_call(
        functools.partial(_ffn_kernel, res_w=res_w, n_in=len(xs), n_out=len(o_specs), n_split=n_split),
        grid=(N, ni, nf),
        in_specs=x_specs + [
            mspec(0), mspec(1), mspec(2),
            pl.BlockSpec((1, D), lambda n, i, f: (0, 0)),
            pl.BlockSpec((1, D), lambda n, i, f: (0, 0)),
            pl.BlockSpec((D, tf), lambda n, i, f: (0, f)),
            pl.BlockSpec((D, tf), lambda n, i, f: (0, f + nf)),
            pl.BlockSpec((tf, D), lambda n, i, f: (f, 0)),
        ],
        out_specs=o_specs,
        out_shape=o_shapes,
        scratch_shapes=[pltpu.VMEM((tm, D), BF16)] + acc_scratch
                       + [pltpu.VMEM((tm, LANES), F32), pltpu.VMEM((2, ROW_CHUNK, D), F32)],
        compiler_params=_cparams(("arbitrary", "arbitrary", "arbitrary")),
        name="ffn",
    )(*xs, mod_l, mod_l, mod_l, g_pre.reshape(1, D), g_post.reshape(1, D), w_in, w_in, w_out)
    return tuple(out) if split_out else out[0]


def _mix_in_kernel(x_ref, sh_ref, sc_ref, gpre_ref, wf_ref, wg_ref, wr_ref, wc_ref,
                   pf_ref, pg_ref, pr_ref, pc_ref):
    x = x_ref[...]
    xn = (_rms(x, gpre_ref[...]) * (1.0 + sc_ref[...]) + sh_ref[...]).astype(BF16)
    pf_ref[...] = _dot(xn, wf_ref[...])
    pg_ref[...] = _dot(xn, wg_ref[...])
    pr_ref[...] = _dot(xn, wr_ref[...])
    pc_ref[...] = _dot(xn, wc_ref[...])


def _mix_in(x, mod_l, g_pre, wf, wg, wr, wc):
    N, S, D = x.shape
    tm = min(512, S)
    sub = 1
    mspec = lambda r: pl.BlockSpec((None, None, 1, D), lambda n, i: (n, 3 * sub + r, 0, 0))
    const = lambda shape: pl.BlockSpec(shape, lambda n, i: (0, 0), pipeline_mode=pl.Buffered(1))
    tok = lambda w: pl.BlockSpec((None, tm, w), lambda n, i: (n, i, 0))
    widths = [w.shape[1] for w in (wf, wg, wr, wc)]
    return pl.pallas_call(
        _mix_in_kernel,
        grid=(N, S // tm),
        in_specs=[tok(D), mspec(0), mspec(1), const((1, D)),
                  const(wf.shape), const(wg.shape), const(wr.shape), const(wc.shape)],
        out_specs=[tok(w) for w in widths],
        out_shape=[jax.ShapeDtypeStruct((N, S, w), F32) for w in widths],
        compiler_params=_cparams(("arbitrary", "arbitrary")),
        name="mix_in",
    )(x, mod_l, mod_l, g_pre.reshape(1, D), wf, wg, wr, wc)


DFT_B = 128
DFT_PAD = 8
DFT_UNROLL = 8


def _fourier_kernel(x_ref, t1_ref, m2_ref, cs_ref, o_ref, vs_ref, ys_ref, *, A):
    B = DFT_B
    pv = A + DFT_PAD
    py = B + DFT_PAD
    GW = FNET_GROUP_DIM

    def stage1(t1, carry):
        xs = x_ref[pl.ds(t1, B, stride=A), :].astype(BF16)
        v = _dot(t1_ref[t1], xs)
        vs_ref[0, pl.ds(t1, B, stride=pv), :] = v[:B]
        vs_ref[1, pl.ds(t1, B, stride=pv), :] = v[B:]
        return carry

    lax.fori_loop(0, A, stage1, 0, unroll=DFT_UNROLL)
    m2 = m2_ref[...]

    def stage2(f2, carry):
        r0 = pl.multiple_of(f2 * pv, 8)
        r = jnp.concatenate([vs_ref[0, pl.ds(r0, A), :], vs_ref[1, pl.ds(r0, A), :]], axis=0)
        y = _dot(m2, r.astype(BF16))
        ys_ref[0, pl.ds(f2, A, stride=py), :] = y[:A]
        ys_ref[1, pl.ds(f2, A, stride=py), :] = y[A:]
        return carry

    lax.fori_loop(0, B, stage2, 0, unroll=DFT_UNROLL)
    c128 = cs_ref[pl.ds(0, GW), :]
    s128 = cs_ref[pl.ds(GW, GW), :]

    def stage3(f1, carry):
        r0 = pl.multiple_of(f1 * py, 8)
        out = _dot(ys_ref[0, pl.ds(r0, B), :].astype(BF16), c128)
        out += _dot(ys_ref[1, pl.ds(r0, B), :].astype(BF16), s128)
        o_ref[pl.ds(pl.multiple_of(f1 * B, B), B), :] = out.astype(o_ref.dtype)
        return carry

    lax.fori_loop(0, A, stage3, 0, unroll=DFT_UNROLL)


def _fourier_tables(S):
    B = DFT_B
    A = S // B
    t1 = np.arange(A)[:, None, None]
    f2 = np.arange(B)[None, :, None]
    t2 = np.arange(B)[None, None, :]
    ang = 2.0 * np.pi * ((f2 * (t1 + A * t2)) % S) / S
    tab1 = np.concatenate([np.cos(ang), -np.sin(ang)], axis=1)
    a = np.arange(A)
    ang_a = 2.0 * np.pi * ((a[:, None] * a[None, :]) % A) / A
    ca, sa = np.cos(ang_a), np.sin(ang_a)
    m2 = np.block([[ca, sa], [-sa, ca]])
    c = np.arange(FNET_GROUP_DIM)
    ang_c = 2.0 * np.pi * ((c[:, None] * c[None, :]) % FNET_GROUP_DIM) / FNET_GROUP_DIM
    scale = (S * FNET_GROUP_DIM) ** -0.5
    cs = np.concatenate([np.cos(ang_c), np.sin(ang_c)], axis=0) * scale
    return jnp.asarray(tab1, BF16), jnp.asarray(m2, BF16), jnp.asarray(cs, BF16)


def _fourier(pf, tables):
    N, S, W = pf.shape
    GW = FNET_GROUP_DIM
    B = DFT_B
    A = S // B
    tab1, m2, cs = tables
    const = lambda a: pl.BlockSpec(a.shape, lambda n, g: (0,) * a.ndim, pipeline_mode=pl.Buffered(1))
    return pl.pallas_call(
        functools.partial(_fourier_kernel, A=A),
        grid=(N, W // GW),
        in_specs=[pl.BlockSpec((None, S, GW), lambda n, g: (n, 0, g)), const(tab1), const(m2), const(cs)],
        out_specs=pl.BlockSpec((None, S, GW), lambda n, g: (n, 0, g)),
        out_shape=jax.ShapeDtypeStruct((N, S, W), BF16),
        scratch_shapes=[pltpu.VMEM((2, B * (A + DFT_PAD), GW), F32),
                        pltpu.VMEM((2, A * (B + DFT_PAD), GW), F32)],
        compiler_params=_cparams(("arbitrary", "arbitrary")),
        name="fourier",
    )(pf, tab1, m2, cs)


def _gla_tables(rev):
    C = GLA_CHUNK
    idx = np.arange(C)
    if not rev:
        cum = (idx[None, :] <= idx[:, None]).astype(np.float32)
    else:
        cum = (idx[None, :] >= idx[:, None]).astype(np.float32)
    mats = []
    m = C // 2
    while m >= 1:
        base = (idx // (2 * m)) * (2 * m)
        piv = base + (m if rev else m - 1)
        mats.append(cum - cum[piv, :])
        m //= 2
    mats.append(cum)
    return np.concatenate(mats, axis=0)


def _split3(x):
    hi = x.astype(BF16)
    r1 = x - hi.astype(F32)
    mid = r1.astype(BF16)
    lo = (r1 - mid.astype(F32)).astype(BF16)
    return hi, mid, lo


def _gla_kernel(*refs, rev, final, nchunks):
    q_ref, k_ref, v_ref, alr_ref = refs[:4]
    og_ref, oth_ref = refs[4:6] if final else (None, None)
    m_ref, wa_ref, ba_ref, g_ref, o_ref, st_ref = refs[-6:]
    C, H, DK, DV = GLA_CHUNK, GLA_HEADS, GLA_DK, GLA_DV
    HK = H * DK

    @pl.when(pl.program_id(1) == 0)
    def _():
        st_ref[...] = jnp.zeros_like(st_ref)

    row = lax.broadcasted_iota(jnp.int32, (C, HK), 0)
    head_of_lane = lax.broadcasted_iota(jnp.int32, (C, HK), 1) >> int(math.log2(DK))
    t_i = lax.broadcasted_iota(jnp.int32, (H * C, C), 0) & (C - 1)
    s_i = lax.broadcasted_iota(jnp.int32, (H * C, C), 1)
    head_of_state_lane = lax.broadcasted_iota(jnp.int32, (DV, HK), 1) >> int(math.log2(DK))
    mstack = m_ref[...]
    d0 = (GLA_RANK if rev else 0)

    def stack_heads(x):
        return jnp.concatenate([jnp.where(head_of_lane == h, x, 0.0) for h in range(H)],
                               axis=0).astype(BF16)

    chunks = [slice(c * C, (c + 1) * C) for c in range(nchunks)]

    a16 = alr_ref[...][:, d0:d0 + GLA_RANK]
    z = jnp.dot(a16, wa_ref[...], preferred_element_type=F32,
                precision=lax.Precision.HIGHEST) + ba_ref[...]
    la = (jnp.minimum(z, 0.0) - jnp.log1p(jnp.exp(-jnp.abs(z)))) / GLA_TAU
    hi, mid, lo = _split3(la)
    e_all = [_dot(mstack, hi[r]) + _dot(mstack, mid[r]) + _dot(mstack, lo[r]) for r in chunks]
    q_all = q_ref[...] * (DK ** -0.5)
    k_all = k_ref[...]
    v_all = v_ref[...].astype(BF16)

    scores = []
    for c, r in enumerate(chunks):
        q, k = q_all[r], k_all[r]
        sc = jnp.zeros((H * C, C), F32)
        m = C // 2
        lvl = 0
        while m >= 1:
            d = e_all[c][lvl * C:(lvl + 1) * C]
            upper = (row & (2 * m - 1)) >= m
            q_act, k_act = (jnp.logical_not(upper), upper) if rev else (upper, jnp.logical_not(upper))
            qh = q * jnp.exp(jnp.where(q_act, d, NEG_BIG))
            kh = (k * jnp.exp(jnp.where(k_act, -d, NEG_BIG))).astype(BF16)
            sh = int(math.log2(2 * m))
            same = (t_i >> sh) == (s_i >> sh)
            sc = sc + jnp.where(same, _dot_nt(stack_heads(qh), kh), 0.0)
            m //= 2
            lvl += 1
        sc = sc + jnp.where(t_i == s_i, _dot_nt(stack_heads(q), k.astype(BF16)), 0.0)
        scores.append(sc.astype(BF16))

    o_intra = [[_dot(scores[c][h * C:(h + 1) * C], v_all[r, h * DV:(h + 1) * DV]) for h in range(H)]
               for c, r in enumerate(chunks)]

    st = st_ref[...]
    for c in (reversed(range(nchunks)) if rev else range(nchunks)):
        r = chunks[c]
        q, k, v = q_all[r], k_all[r], v_all[r]
        b = e_all[c][6 * C:7 * C]
        b_edge = b[0:1, :] if rev else b[C - 1:C, :]
        kd = (k * jnp.exp(b_edge - b)).astype(BF16)
        o_inter = _dot_nt(stack_heads(q * jnp.exp(b)), st.astype(BF16))
        upd = _dot_tn(v, kd)
        st = st * jnp.exp(b_edge)
        for h in range(H):
            st = st + jnp.where(head_of_state_lane == h, upd[h * DV:(h + 1) * DV], 0.0)
        outs = []
        for h in range(H):
            o = o_intra[c][h] + o_inter[h * C:(h + 1) * C]
            if final:
                tot = o + oth_ref[r, h * DV:(h + 1) * DV]
                y = _rms(tot, g_ref[...]) * _silu(og_ref[r, h * DV:(h + 1) * DV])
                outs.append(y.astype(o_ref.dtype))
            else:
                outs.append(o)
        o_ref[r, :] = jnp.concatenate(outs, axis=-1)
    st_ref[...] = st


def _gla_pass(pg, other, w_alpha, b_alpha, norm_g, *, rev, final):
    N, S, _ = pg.shape
    HK, HV = GLA_HEADS * GLA_DK, GLA_HEADS * GLA_DV
    tb = min(256, S)
    nb = S // tb
    blk = (lambda i: nb - 1 - i) if rev else (lambda i: i)
    tok = lambda w, j: pl.BlockSpec((None, tb, w), lambda n, i: (n, blk(i), j))
    const = lambda shape: pl.BlockSpec(shape, lambda n, i: (0,) * len(shape))
    mstack = jnp.asarray(_gla_tables(rev), BF16)
    d = 1 if rev else 0
    out_dtype = BF16 if final else F32
    extra_specs = [tok(HV, 2), tok(HV, 0)] if final else []
    extra_args = (pg, other) if final else ()
    return pl.pallas_call(
        functools.partial(_gla_kernel, rev=rev, final=final, nchunks=tb // GLA_CHUNK),
        grid=(N, nb),
        in_specs=[tok(HK, 0), tok(HK, 1), tok(HV, 1), tok(128, (2 * HK + 2 * HV) // 128)]
                 + extra_specs
                 + [const(mstack.shape), const((GLA_RANK, HK)), const((1, HK)), const((1, GLA_DV))],
        out_specs=pl.BlockSpec((None, tb, HV), lambda n, i: (n, blk(i), 0)),
        out_shape=jax.ShapeDtypeStruct((N, S, HV), out_dtype),
        scratch_shapes=[pltpu.VMEM((GLA_DV, HK), F32)],
        compiler_params=_cparams(("arbitrary", "arbitrary")),
        name="gla_rev" if rev else "gla_fwd",
    )(pg, pg, pg, pg, *extra_args, mstack, w_alpha[d], b_alpha[d].reshape(1, HK),
      norm_g.reshape(1, GLA_DV))


def _softplus(x):
    return jnp.maximum(x, 0.0) + jnp.log1p(jnp.exp(-jnp.abs(x)))


def _gelu_tanh(x):
    return 0.5 * x * (1.0 + jnp.tanh(math.sqrt(2.0 / math.pi) * (x + 0.044715 * (x * x * x))))


def _lru_kernel(*refs, rev, final, tb):
    x_ref, halo_ref = refs[:2]
    rg_ref, oth_ref = refs[2:4] if final else (None, None)
    cw_ref, cb_ref, wa_ref, ba_ref, wx_ref, bx_ref, lam_ref, o_ref, ext_ref, car_ref = refs[-10:]
    W = x_ref.shape[-1]
    i = pl.program_id(1)
    first = i == 0

    @pl.when(first)
    def _():
        car_ref[...] = jnp.zeros_like(car_ref)

    halo = jnp.where(first, 0.0, halo_ref[...])
    if rev:
        ext_ref[pl.ds(0, tb), :] = x_ref[...]
        ext_ref[pl.ds(tb, LRU_HALO), :] = halo
    else:
        ext_ref[pl.ds(0, LRU_HALO), :] = halo
        ext_ref[pl.ds(LRU_HALO, tb), :] = x_ref[...]
    cw = cw_ref[...]
    xc = jnp.zeros((tb, W), F32) + cb_ref[...]
    for tap in range(LRU_TAPS):
        off = (LRU_TAPS - 1 - tap) if rev else (LRU_HALO - (LRU_TAPS - 1) + tap)
        xc = xc + cw[tap:tap + 1, :] * ext_ref[pl.ds(off, tb), :]

    xcb = xc.astype(BF16)
    r = _sigmoid_tanh(_dot(xcb, wa_ref[...]) + ba_ref[...])
    g = _sigmoid_tanh(_dot(xcb, wx_ref[...]) + bx_ref[...])
    log_a = (-LRU_C) * r * _softplus(-lam_ref[...])
    a = jnp.exp(log_a)
    gain2 = 1.0 - jnp.exp(2.0 * log_a)
    h = jnp.where(gain2 > 0.0, gain2 * lax.rsqrt(gain2), 0.0) * (g * xc)

    G = 8
    pos = lax.broadcasted_iota(jnp.int32, (tb, W), 0) & (G - 1)
    d = 1
    while d < G:
        if rev:
            valid = pos < G - d
            shift = tb - d
        else:
            valid = pos >= d
            shift = d
        a_s = jnp.where(valid, pltpu.roll(a, shift, 0), 1.0)
        h_s = jnp.where(valid, pltpu.roll(h, shift, 0), 0.0)
        h = h + a * h_s
        a = a * a_s
        d *= 2
    carry = car_ref[...]
    groups = [None] * (tb // G)
    for g in (reversed(range(tb // G)) if rev else range(tb // G)):
        hg = h[g * G:(g + 1) * G] + a[g * G:(g + 1) * G] * carry
        carry = hg[0:1, :] if rev else hg[G - 1:G, :]
        groups[g] = hg
    h = jnp.concatenate(groups, axis=0)
    car_ref[...] = carry
    if final:
        o_ref[...] = ((h + oth_ref[...]) * _gelu_tanh(rg_ref[...])).astype(o_ref.dtype)
    else:
        o_ref[...] = h


def _lru_pass(pr, other, conv_w, conv_b, wa_bd, b_a, wx_bd, b_x, lam, *, rev, final):
    N, S, _ = pr.shape
    W = GROUP_WIDTH
    tb = min(256, S)
    nb = S // tb
    hb = tb // LRU_HALO
    blk = (lambda i: nb - 1 - i) if rev else (lambda i: i)
    if rev:
        halo_blk = lambda i: jnp.minimum((blk(i) + 1) * hb, S // LRU_HALO - 1)
    else:
        halo_blk = lambda i: jnp.maximum(blk(i) * hb - 1, 0)
    tok = lambda j: pl.BlockSpec((None, tb, W), lambda n, i: (n, blk(i), j))
    const = lambda shape: pl.BlockSpec(shape, lambda n, i: (0,) * len(shape))
    out_dtype = BF16 if final else F32
    d = 1 if rev else 0
    extra_specs = [tok(1), tok(0)] if final else []
    extra_args = (pr, other) if final else ()
    return pl.pallas_call(
        functools.partial(_lru_kernel, rev=rev, final=final, tb=tb),
        grid=(N, nb),
        in_specs=[tok(0),
                  pl.BlockSpec((None, LRU_HALO, W), lambda n, i: (n, halo_blk(i), 0))]
                 + extra_specs
                 + [const((LRU_TAPS, W)), const((1, W)), const((W, W)), const((1, W)),
                    const((W, W)), const((1, W)), const((1, W))],
        out_specs=pl.BlockSpec((None, tb, W), lambda n, i: (n, blk(i), 0)),
        out_shape=jax.ShapeDtypeStruct((N, S, W), out_dtype),
        scratch_shapes=[pltpu.VMEM((tb + LRU_HALO, W), F32), pltpu.VMEM((1, W), F32)],
        compiler_params=_cparams(("arbitrary", "arbitrary")),
        name="lru_rev" if rev else "lru_fwd",
    )(pr, pr, *extra_args, conv_w[d], conv_b[d].reshape(1, W), wa_bd[d], b_a[d].reshape(1, W),
      wx_bd[d], b_x[d].reshape(1, W), lam[d].reshape(1, W))


def _conf_kernel(x_ref, prev_ref, next_ref, cw_ref, cb_ref, lg_ref, lb_ref, o_ref, ext_ref, sh_ref, *, tb):
    W = GROUP_WIDTH
    i = pl.program_id(1)
    last = pl.num_programs(1) - 1

    def glu(t):
        return t[:, :W] * _sigmoid_tanh(t[:, W:])

    ext_ref[pl.ds(0, CONF_HALO), :] = jnp.where(i == 0, 0.0, glu(prev_ref[...]))
    ext_ref[pl.ds(CONF_HALO, tb), :] = glu(x_ref[...])
    ext_ref[pl.ds(CONF_HALO + tb, CONF_HALO), :] = jnp.where(i == last, 0.0, glu(next_ref[...]))
    span = tb + 2 * CONF_HALO - 8
    for r in range(1, 8):
        sh_ref[r - 1] = ext_ref[pl.ds(r, span), :]
    cw = cw_ref[...]
    u = jnp.zeros((tb, W), F32) + cb_ref[...]
    base = CONF_HALO - CONF_TAPS // 2
    for tap in range(CONF_TAPS):
        q, r = divmod(base + tap, 8)
        src = ext_ref[pl.ds(8 * q, tb), :] if r == 0 else sh_ref[r - 1, pl.ds(8 * q, tb), :]
        u = u + cw[tap:tap + 1, :] * src
    uc = u - jnp.mean(u, axis=-1, keepdims=True)
    y = uc * lax.rsqrt(jnp.mean(uc * uc, axis=-1, keepdims=True) + EPS) * lg_ref[...] + lb_ref[...]
    o_ref[...] = (y * _sigmoid_tanh(y)).astype(o_ref.dtype)


def _conformer(pc, dw_w, dw_b, ln_g, ln_b):
    N, S, W2 = pc.shape
    W = GROUP_WIDTH
    tb = min(256, S)
    nb = S // tb
    hb = tb // CONF_HALO
    taps_pad = 32
    cw = jnp.zeros((taps_pad, W), F32).at[:CONF_TAPS].set(dw_w)
    const = lambda shape: pl.BlockSpec(shape, lambda n, i: (0,) * len(shape))
    return pl.pallas_call(
        functools.partial(_conf_kernel, tb=tb),
        grid=(N, nb),
        in_specs=[pl.BlockSpec((None, tb, W2), lambda n, i: (n, i, 0)),
                  pl.BlockSpec((None, CONF_HALO, W2), lambda n, i: (n, jnp.maximum(i * hb - 1, 0), 0)),
                  pl.BlockSpec((None, CONF_HALO, W2),
                               lambda n, i: (n, jnp.minimum((i + 1) * hb, S // CONF_HALO - 1), 0)),
                  const((taps_pad, W)), const((1, W)), const((1, W)), const((1, W))],
        out_specs=pl.BlockSpec((None, tb, W), lambda n, i: (n, i, 0)),
        out_shape=jax.ShapeDtypeStruct((N, S, W), BF16),
        scratch_shapes=[pltpu.VMEM((tb + 2 * CONF_HALO, W), F32),
                        pltpu.VMEM((7, tb + 2 * CONF_HALO - 8, W), F32)],
        compiler_params=_cparams(("arbitrary", "arbitrary")),
        name="conformer",
    )(pc, pc, pc, cw, dw_b.reshape(1, W), ln_g.reshape(1, W), ln_b.reshape(1, W))


def _mix_out_kernel(x_ref, gt_ref, gpost_ref, yf_ref, yg_ref, yr_ref, yc_ref, w_ref, o_ref):
    W = GROUP_WIDTH
    acc = _dot(yf_ref[...], w_ref[pl.ds(0, W), :])
    acc += _dot(yg_ref[...], w_ref[pl.ds(W, W), :])
    acc += _dot(yr_ref[...], w_ref[pl.ds(2 * W, W), :])
    acc += _dot(yc_ref[...], w_ref[pl.ds(3 * W, W), :])
    o_ref[...] = x_ref[...] + gt_ref[...] * _rms(acc, gpost_ref[...])


def _mix_out(x, mod_l, g_post, y_f, y_g, y_r, y_c, w_out):
    N, S, D = x.shape
    W = GROUP_WIDTH
    tm = min(512, S)
    sub = 1
    tok = lambda w: pl.BlockSpec((None, tm, w), lambda n, i: (n, i, 0))
    return pl.pallas_call(
        _mix_out_kernel,
        grid=(N, S // tm),
        in_specs=[tok(D),
                  pl.BlockSpec((None, None, 1, D), lambda n, i: (n, 3 * sub + 2, 0, 0)),
                  pl.BlockSpec((1, D), lambda n, i: (0, 0)),
                  tok(W), tok(W), tok(W), tok(W),
                  pl.BlockSpec(w_out.shape, lambda n, i: (0, 0), pipeline_mode=pl.Buffered(1))],
        out_specs=tok(D),
        out_shape=jax.ShapeDtypeStruct((N, S, D), F32),
        compiler_params=_cparams(("arbitrary", "arbitrary")),
        name="mix_out",
    )(x, mod_l, g_post.reshape(1, D), y_f, y_g, y_r, y_c, w_out)


def _block_diag(w):
    nd, nblk, bd, _ = w.shape
    out = jnp.zeros((nd, nblk * bd, nblk * bd), w.dtype)
    for b in range(nblk):
        out = out.at[:, b * bd:(b + 1) * bd, b * bd:(b + 1) * bd].set(w[:, b])
    return out


def kernel(x_prompt, x_sample, c_prompt, c_sample, w_ada, b_ada, g_pre, g_post, ffn1_w_in, ffn1_w_out, ffn2_w_in, ffn2_w_out, w_mix_in, gla_w_alpha, gla_b_alpha, gla_norm_g, lru_conv_w, lru_conv_b, lru_w_a, lru_b_a, lru_w_x, lru_b_x, lru_lambda, conf_dw_w, conf_dw_b, conf_ln_g, conf_ln_b, w_mix_out):
    assert x_prompt.shape[1:] == x_sample.shape[1:]
    B0 = x_prompt.shape[0]
    N = B0 + x_sample.shape[0]
    S, D = x_prompt.shape[1:]
    L = w_ada.shape[0]
    NP = -(-N // 8) * 8
    c = jnp.zeros((NP, D), F32).at[:N].set(jnp.concatenate([c_prompt, c_sample], axis=0))

    mod = _adaln_mod(c, w_ada, b_ada).reshape(L, NP, 3 * N_SUB, 1, D)
    dft_tables = _fourier_tables(S)
    x = (x_prompt, x_sample)

    W = GROUP_WIDTH
    HK, HV = GLA_HEADS * GLA_DK, GLA_HEADS * GLA_DV
    o_q = W
    o_k = o_q + HK
    o_v = o_k + HK
    o_og = o_v + HV
    o_alr = o_og + HV
    o_rin = o_alr + 2 * GLA_RANK
    o_rg = o_rin + W
    o_c = o_rg + W

    for l in range(L):
        wmi = w_mix_in[l]
        wf = wmi[:, :W].astype(BF16)
        wg = jnp.concatenate([wmi[:, o_q:o_alr], wmi[:, o_alr:o_rin],
                              jnp.zeros((D, 128 - 2 * GLA_RANK), F32)], axis=1).astype(BF16)
        wr = wmi[:, o_rin:o_c].astype(BF16)
        wc = wmi[:, o_c:].astype(BF16)

        x = _ffn(x, mod[l], 0, g_pre[l, 0], g_post[l, 0],
                 ffn1_w_in[l].astype(BF16), ffn1_w_out[l].astype(BF16), 0.5)

        pf, pg, pr, pc = _mix_in(x, mod[l], g_pre[l, 1], wf, wg, wr, wc)
        y_f = _fourier(pf, dft_tables)

        o_rev = _gla_pass(pg, None, gla_w_alpha[l], gla_b_alpha[l], gla_norm_g[l], rev=True, final=False)
        y_g = _gla_pass(pg, o_rev, gla_w_alpha[l], gla_b_alpha[l], gla_norm_g[l], rev=False, final=True)

        wa_bd = _block_diag(lru_w_a[l]).astype(BF16)
        wx_bd = _block_diag(lru_w_x[l]).astype(BF16)
        lru_args = (lru_conv_w[l], lru_conv_b[l], wa_bd, lru_b_a[l], wx_bd, lru_b_x[l], lru_lambda[l])
        h_rev = _lru_pass(pr, None, *lru_args, rev=True, final=False)
        y_r = _lru_pass(pr, h_rev, *lru_args, rev=False, final=True)

        y_c = _conformer(pc, conf_dw_w[l], conf_dw_b[l], conf_ln_g[l], conf_ln_b[l])

        x = _mix_out(x, mod[l], g_post[l, 1], y_f, y_g, y_r, y_c, w_mix_out[l].astype(BF16))

        x = _ffn(x, mod[l], 2, g_pre[l, 2], g_post[l, 2],
                 ffn2_w_in[l].astype(BF16), ffn2_w_out[l].astype(BF16), 0.5,
                 split_out=B0 if l == L - 1 else None)

    return tuple(x)
```
